```python
import math
import jax, jax.numpy as jnp
from jax import lax
import numpy as np


D_MODEL = 1024
BATCH = 8
SEQ = 2048
DEPTH = 2
DEC_BATCH = 128
DEC_SEQ = 8
PAST_LEN = 16384
PAGE_SIZE = 128

MIX_WIDTH = D_MODEL
POOL_WIDTH = MIX_WIDTH // 2
POOL_WINDOWS = (2, 4, 8, 16)
N_POOL_GROUPS = len(POOL_WINDOWS)
POOL_GROUP = POOL_WIDTH // N_POOL_GROUPS
POOL_BUF = max(POOL_WINDOWS) - 1
RET_WIDTH = MIX_WIDTH - POOL_WIDTH
RET_HEADS = 4
RET_HEAD_DIM = RET_WIDTH // RET_HEADS
RET_CHUNK = 128
ROPE_BASE = 10000.0
IN_COLS = POOL_WIDTH + 4 * RET_WIDTH
N_KEYS = 128
N_EXPERTS = N_KEYS * N_KEYS
PEER_HEADS = 8
PEER_TOPK = 16
PEER_KEY_DIM = 256
PEER_HALF = PEER_KEY_DIM // 2
PEER_BLOCK = 256
EPS = 1e-6

kernel_name = 'pool_retention_peer_decode_step'


def rmsnorm(x, g):
    xf = x.astype(jnp.float32)
    y = xf * lax.rsqrt(jnp.mean(xf * xf, axis=-1, keepdims=True) + EPS)
    return (y * g.astype(jnp.float32)).astype(x.dtype)


def pool_mix(u, buf, pos0, pool_w, pool_scale):
    B, L, _ = u.shape
    full_raw = jnp.concatenate([buf.astype(u.dtype), u], axis=1)
    full = full_raw.astype(jnp.float32)
    cs = jnp.cumsum(full, axis=1)
    cs = jnp.concatenate([jnp.zeros_like(cs[:, :1]), cs], axis=1)
    end = cs[:, POOL_BUF + 1:]
    pos = pos0 + jnp.arange(L, dtype=jnp.int32)
    outs = []
    for g, w in enumerate(POOL_WINDOWS):
        sl = slice(g * POOL_GROUP, (g + 1) * POOL_GROUP)
        start = cs[:, POOL_BUF + 1 - w: POOL_BUF + 1 - w + L, sl]
        cnt = jnp.minimum(pos + 1, w).astype(jnp.float32)[None, :, None]
        outs.append((end[..., sl] - start) / cnt)
    pooled = jnp.concatenate(outs, axis=-1) - full[:, POOL_BUF:]
    pooled = pooled.reshape(B, L, N_POOL_GROUPS, POOL_GROUP)
    mixed = jnp.einsum('blgc,gcd->blgd', pooled, pool_w.astype(jnp.float32))
    out = mixed.reshape(B, L, POOL_WIDTH) * pool_scale.astype(jnp.float32)
    new_buf = full_raw[:, -POOL_BUF:]
    return out, new_buf


def rotary(x, pos):
    half = x.shape[-1] // 2
    inv = ROPE_BASE ** (-jnp.arange(half, dtype=jnp.float32) / half)
    ang = pos.astype(jnp.float32)[:, None] * inv[None, :]
    cos, sin = jnp.cos(ang), jnp.sin(ang)
    x1, x2 = x[..., :half], x[..., half:]
    return jnp.concatenate([x1 * cos - x2 * sin, x1 * sin + x2 * cos], axis=-1)


def retention(q, k, v, s0):
    B, H, L, d = q.shape
    C = math.gcd(L, RET_CHUNK)
    n = L // C
    gamma = 1.0 - 2.0 ** (-5.0 - jnp.arange(H, dtype=jnp.float32))
    log_g = jnp.log(gamma)
    idx = jnp.arange(C, dtype=jnp.float32)
    diff = idx[:, None] - idx[None, :]
    decay = jnp.where(diff >= 0, jnp.exp(log_g[:, None, None] * jnp.maximum(diff, 0.0)), 0.0)
    qc = q.reshape(B, H, n, C, d)
    kc = k.reshape(B, H, n, C, d)
    vc = v.reshape(B, H, n, C, d)
    scores = jnp.einsum('bhnid,bhnjd->bhnij', qc, kc) * decay[None, :, None]
    intra = jnp.einsum('bhnij,bhnje->bhnie', scores, vc)
    zeta = jnp.exp(log_g[:, None] * (C - 1 - idx)[None, :])
    kv = jnp.einsum('bhnjd,hj,bhnje->bhnde', kc, zeta, vc)
    g_C = jnp.exp(log_g * C)[None, :, None, None]

    def step(S, kv_n):
        return g_C * S + kv_n, S

    S_final, S_prev = lax.scan(step, s0, jnp.moveaxis(kv, 2, 0))
    S_prev = jnp.moveaxis(S_prev, 0, 2)
    xi = jnp.exp(log_g[:, None] * (idx + 1.0)[None, :])
    cross = jnp.einsum('bhnid,bhnde->bhnie', qc, S_prev) * xi[None, :, None, :, None]
    return (intra + cross).reshape(B, H, L, d), S_final


def mixing_layer(x, buf, s0, pos0, norm_g, w_in, pool_w, pool_scale, w_out):
    B, L, _ = x.shape
    h = rmsnorm(x, norm_g)
    proj = h @ w_in
    u = proj[..., :POOL_WIDTH]
    o1 = POOL_WIDTH
    q = proj[..., o1: o1 + RET_WIDTH]
    k = proj[..., o1 + RET_WIDTH: o1 + 2 * RET_WIDTH]
    v = proj[..., o1 + 2 * RET_WIDTH: o1 + 3 * RET_WIDTH]
    gate = proj[..., o1 + 3 * RET_WIDTH:]
    pool_out, new_buf = pool_mix(u, buf, pos0, pool_w, pool_scale)
    pos = pos0 + jnp.arange(L, dtype=jnp.int32)

    def heads(t):
        return t.astype(jnp.float32).reshape(B, L, RET_HEADS, RET_HEAD_DIM).transpose(0, 2, 1, 3)

    qh = rotary(heads(q), pos)
    kh = rotary(heads(k), pos) * (RET_HEAD_DIM ** -0.5)
    vh = heads(v)
    o, S = retention(qh, kh, vh, s0.astype(jnp.float32))
    o = o * lax.rsqrt(jnp.mean(o * o, axis=-1, keepdims=True) + EPS)
    o = o.transpose(0, 2, 1, 3).reshape(B, L, RET_WIDTH)
    ret_out = jax.nn.silu(gate.astype(jnp.float32)) * o
    mix = jnp.concatenate([pool_out, ret_out], axis=-1).astype(x.dtype) @ w_out
    return x + mix, new_buf, S


def peer(h, wq, keys, u_tab, v_tab):
    B, L, D = h.shape
    T = B * L
    t = h.reshape(T, D)
    q = (t @ wq).astype(jnp.float32).reshape(T, PEER_HEADS, 2, PEER_HALF)
    s = jnp.einsum('thpc,pkc->thpk', q, keys.astype(jnp.float32))
    s_top, i_top = lax.top_k(s, PEER_TOPK)
    cand = s_top[..., 0, :, None] + s_top[..., 1, None, :]
    cand = cand.reshape(T, PEER_HEADS, PEER_TOPK * PEER_TOPK)
    best, bi = lax.top_k(cand, PEER_TOPK)
    i1 = jnp.take_along_axis(i_top[..., 0, :], bi // PEER_TOPK, axis=-1)
    i2 = jnp.take_along_axis(i_top[..., 1, :], bi % PEER_TOPK, axis=-1)
    experts = (i1 * N_KEYS + i2).reshape(T, PEER_HEADS * PEER_TOPK)
    gates = jax.nn.softmax(best, axis=-1).reshape(T, PEER_HEADS * PEER_TOPK)
    nb = -(-T // PEER_BLOCK)
    pad = nb * PEER_BLOCK - T
    t_p = jnp.pad(t, ((0, pad), (0, 0))).reshape(nb, PEER_BLOCK, D)
    e_p = jnp.pad(experts, ((0, pad), (0, 0))).reshape(nb, PEER_BLOCK, -1)
    g_p = jnp.pad(gates, ((0, pad), (0, 0))).reshape(nb, PEER_BLOCK, -1)

    def block(args):
        tb, eb, gb = args
        ub = u_tab[eb]
        vb = v_tab[eb]
        a = jax.nn.gelu(jnp.einsum('bd,bed->be', tb, ub).astype(jnp.float32), approximate=False)
        return jnp.einsum('be,bed->bd', (gb * a).astype(tb.dtype), vb)

    out = lax.map(block, (t_p, e_p, g_p))
    return out.reshape(nb * PEER_BLOCK, D)[:T].reshape(B, L, D)


def trunk_layer(x, buf, s0, pos0, l, norm_mix, w_in, pool_w, pool_scale, w_out,
                norm_ffn, peer_wq, peer_keys, peer_u, peer_v):
    x, new_buf, S = mixing_layer(x, buf, s0, pos0, norm_mix[l], w_in[l], pool_w[l],
                                 pool_scale[l], w_out[l])
    x = x + peer(rmsnorm(x, norm_ffn[l]), peer_wq[l], peer_keys[l], peer_u[l], peer_v[l])
    return x, new_buf, S


def setup_inputs(seed: int = 0) -> dict:
    key = jax.random.key(seed)
    ks = jax.random.split(key, 16)
    f32 = jnp.float32
    nrm = lambda k, shape, s: jax.random.normal(k, shape, f32) * s
    return {
        'x_prompt': nrm(ks[0], (BATCH, SEQ, D_MODEL), 1.0),
        'x_sample': nrm(ks[1], (DEC_BATCH, DEC_SEQ, D_MODEL), 1.0),
        'state_pool': nrm(ks[2], (DEPTH, DEC_BATCH, POOL_BUF, POOL_WIDTH), 1.0),
        'state_ret': nrm(ks[3], (DEPTH, DEC_BATCH, RET_HEADS, RET_HEAD_DIM, RET_HEAD_DIM), 0.5),
        'norm_mix': 1.0 + nrm(ks[4], (DEPTH, D_MODEL), 0.02),
        'w_in': nrm(ks[5], (DEPTH, D_MODEL, IN_COLS), D_MODEL ** -0.5),
        'pool_w': nrm(ks[6], (DEPTH, N_POOL_GROUPS, POOL_GROUP, POOL_GROUP), POOL_GROUP ** -0.5),
        'pool_scale': 1.0 + nrm(ks[7], (DEPTH, POOL_WIDTH), 0.02),
        'w_out': nrm(ks[8], (DEPTH, MIX_WIDTH, D_MODEL), MIX_WIDTH ** -0.5),
        'norm_ffn': 1.0 + nrm(ks[9], (DEPTH, D_MODEL), 0.02),
        'peer_wq': nrm(ks[10], (DEPTH, D_MODEL, PEER_HEADS * PEER_KEY_DIM), D_MODEL ** -0.5),
        'peer_keys': nrm(ks[11], (DEPTH, 2, N_KEYS, PEER_HALF), PEER_HALF ** -0.5),
        'peer_u': nrm(ks[12], (DEPTH, N_EXPERTS, D_MODEL), D_MODEL ** -0.5),
        'peer_v': nrm(ks[13], (DEPTH, N_EXPERTS, D_MODEL), 0.3),
        'norm_final': 1.0 + nrm(ks[14], (D_MODEL,), 0.02),
    }


def reference(x_prompt, x_sample, state_pool, state_ret, norm_mix, w_in, pool_w,
              pool_scale, w_out, norm_ffn, peer_wq, peer_keys, peer_u, peer_v, norm_final):
    yp, ys = x_prompt, x_sample
    bp = x_prompt.shape[0]
    pool_p, ret_p, pool_s, ret_s = [], [], [], []
    for l in range(DEPTH):
        zero_buf = jnp.zeros((bp, POOL_BUF, POOL_WIDTH), x_prompt.dtype)
        zero_S = jnp.zeros((bp, RET_HEADS, RET_HEAD_DIM, RET_HEAD_DIM), jnp.float32)
        yp, nbp, nsp = trunk_layer(yp, zero_buf, zero_S, 0, l, norm_mix, w_in, pool_w,
                                   pool_scale, w_out, norm_ffn, peer_wq, peer_keys, peer_u, peer_v)
        ys, nbs, nss = trunk_layer(ys, state_pool[l], state_ret[l], PAST_LEN, l, norm_mix, w_in,
                                   pool_w, pool_scale, w_out, norm_ffn, peer_wq, peer_keys,
                                   peer_u, peer_v)
        pool_p.append(nbp.astype(state_pool.dtype))
        ret_p.append(nsp.astype(state_ret.dtype))
        pool_s.append(nbs.astype(state_pool.dtype))
        ret_s.append(nss.astype(state_ret.dtype))
    y_prompt = rmsnorm(yp, norm_final)
    y_sample = rmsnorm(ys, norm_final)
    new_pool_prompt = jnp.stack(pool_p)
    new_ret_prompt = jnp.stack(ret_p)
    new_pool_sample = jnp.stack(pool_s)
    new_ret_sample = jnp.stack(ret_s)
    return (y_prompt, y_sample, new_pool_prompt, new_ret_prompt, new_pool_sample, new_ret_sample)
```

```python
import functools
import math

import numpy as np
import jax
import jax.numpy as jnp
from jax import lax
from jax.experimental import pallas as pl
from jax.experimental.pallas import tpu as pltpu

D_MODEL = 1024
POOL_WIDTH = 512
POOL_WINDOWS = (2, 4, 8, 16)
POOL_GROUP = 128
POOL_BUF = 15
POOL_PAD = 16
RET_WIDTH = 512
RET_HEADS = 4
RET_HEAD_DIM = 128
RET_CHUNK = 128
ROPE_BASE = 10000.0
IN_COLS = POOL_WIDTH + 4 * RET_WIDTH
N_KEYS = 128
PEER_HEADS = 8
PEER_TOPK = 16
PEER_HALF = 128
N_TERMS = PEER_HEADS * PEER_TOPK
EPS = 1e-6
PAST_LEN = 16384

F32 = jnp.float32
MXU_DTYPE = jnp.bfloat16
NEG_INF = float("-inf")
W_PITCH = 136
VMEM_LIMIT = 56 * 1024 * 1024


def _tile(n, pref):
    t = min(n, pref)
    while n % t:
        t //= 2
    return t


def _params(sem):
    return pltpu.CompilerParams(dimension_semantics=sem, vmem_limit_bytes=VMEM_LIMIT)


def _rms(x, g):
    ms = jnp.mean(x * x, axis=-1, keepdims=True)
    return x * lax.rsqrt(ms + EPS) * g


def _norm_matmul_kernel(x_ref, g_ref, w_ref, o_ref):
    h = _rms(x_ref[...], g_ref[...])
    o_ref[...] = jnp.dot(h.astype(MXU_DTYPE), w_ref[...], preferred_element_type=F32)


def _norm_matmul(x, g, w):
    t, d = x.shape
    n = w.shape[1]
    tm = _tile(t, 256)
    return pl.pallas_call(
        _norm_matmul_kernel,
        grid=(t // tm,),
        in_specs=[pl.BlockSpec((tm, d), lambda i: (i, 0)),
                  pl.BlockSpec((1, d), lambda i: (0, 0)),
                  pl.BlockSpec((d, n), lambda i: (0, 0))],
        out_specs=pl.BlockSpec((tm, n), lambda i: (i, 0)),
        out_shape=jax.ShapeDtypeStruct((t, n), F32),
        compiler_params=_params(("parallel",)),
        name="norm_matmul",
    )(x, g, w)


def _rotary(x, cosf, sinf):
    return x * cosf + pltpu.roll(x, RET_HEAD_DIM // 2, 1) * sinf


def _head_cols(proj_ref, part, h):
    lo = POOL_WIDTH + part * RET_WIDTH + h * RET_HEAD_DIM
    return proj_ref[:, lo:lo + RET_HEAD_DIM]


def _head_norm_gate(o, gate):
    on = o * lax.rsqrt(jnp.mean(o * o, axis=-1, keepdims=True) + EPS)
    return gate * jax.nn.sigmoid(gate) * on


def _nt_dot(a, b):
    return lax.dot_general(a, b, (((1,), (1,)), ((), ())), preferred_element_type=F32)


def _tn_dot(a, b):
    return lax.dot_general(a, b, (((0,), (0,)), ((), ())), preferred_element_type=F32)


def _mix_prompt_kernel(proj_ref, x_ref, cos_ref, sin_ref, decay_ref, zeta_ref, xi_ref,
                       poolw_ref, pscale_ref, wout_ref,
                       y_ref, newbuf_ref, news_ref,
                       ext_ref, s_ref, mix_ref, *, gc, pos0):
    c = pl.program_id(1)
    C = RET_CHUNK

    @pl.when(c == 0)
    def _():
        ext_ref[0:POOL_PAD, :] = jnp.zeros((POOL_PAD, POOL_WIDTH), F32)
        s_ref[...] = jnp.zeros(s_ref.shape, F32)

    ext_ref[POOL_PAD:POOL_PAD + C, :] = proj_ref[:, 0:POOL_WIDTH]
    pos = pos0 + c * C + lax.broadcasted_iota(jnp.int32, (C, POOL_GROUP), 0)
    for g, w in enumerate(POOL_WINDOWS):
        sl = slice(g * POOL_GROUP, (g + 1) * POOL_GROUP)
        tok = ext_ref[POOL_PAD:POOL_PAD + C, sl]
        acc = tok
        for j in range(1, w):
            acc = acc + ext_ref[POOL_PAD - j:POOL_PAD - j + C, sl]
        cnt = jnp.minimum(pos + 1, w).astype(F32)
        pooled = acc / cnt - tok
        mixed = jnp.dot(pooled.astype(MXU_DTYPE), poolw_ref[g], preferred_element_type=F32)
        mix_ref[:, sl] = (mixed * pscale_ref[:, sl]).astype(MXU_DTYPE)
    ext_ref[0:POOL_PAD, :] = ext_ref[C:C + POOL_PAD, :]

    cosf = cos_ref[...]
    sinf = sin_ref[...]
    for h in range(RET_HEADS):
        q = _rotary(_head_cols(proj_ref, 0, h), cosf, sinf)
        k = _rotary(_head_cols(proj_ref, 1, h), cosf, sinf) * (RET_HEAD_DIM ** -0.5)
        vb = _head_cols(proj_ref, 2, h).astype(MXU_DTYPE)
        gate = _head_cols(proj_ref, 3, h)
        qb = q.astype(MXU_DTYPE)
        scores = _nt_dot(qb, k.astype(MXU_DTYPE)) * decay_ref[h]
        intra = jnp.dot(scores.astype(MXU_DTYPE), vb, preferred_element_type=F32)
        s_prev = s_ref[h]
        cross = jnp.dot(qb, s_prev.astype(MXU_DTYPE), preferred_element_type=F32) * xi_ref[h]
        kv = _tn_dot((k * zeta_ref[h]).astype(MXU_DTYPE), vb)
        s_ref[h] = gc[h] * s_prev + kv
        lo = POOL_WIDTH + h * RET_HEAD_DIM
        mix_ref[:, lo:lo + RET_HEAD_DIM] = _head_norm_gate(intra + cross, gate).astype(MXU_DTYPE)

    y_ref[...] = x_ref[...] + jnp.dot(mix_ref[...], wout_ref[...], preferred_element_type=F32)

    @pl.when(c == pl.num_programs(1) - 1)
    def _():
        newbuf_ref[0] = ext_ref[1:POOL_PAD, :]
        news_ref[0] = s_ref[...]


def _mix_prompt(proj, x, tabs, poolw, pscale, wout, batch, seq):
    C = RET_CHUNK
    nc = seq // C
    full = lambda shape: pl.BlockSpec(shape, lambda b, c: (0,) * len(shape))
    row = lambda width: pl.BlockSpec((C, width), lambda b, c: (b * nc + c, 0))
    kern = functools.partial(_mix_prompt_kernel, gc=tabs["gc"], pos0=0)
    return pl.pallas_call(
        kern,
        grid=(batch, nc),
        in_specs=[row(IN_COLS), row(D_MODEL),
                  pl.BlockSpec((C, RET_HEAD_DIM), lambda b, c: (c, 0)),
                  pl.BlockSpec((C, RET_HEAD_DIM), lambda b, c: (c, 0)),
                  full((RET_HEADS, C, C)), full((RET_HEADS, C, RET_HEAD_DIM)),
                  full((RET_HEADS, C, RET_HEAD_DIM)),
                  full((len(POOL_WINDOWS), POOL_GROUP, POOL_GROUP)), full((1, POOL_WIDTH)),
                  full((D_MODEL, D_MODEL))],
        out_specs=[row(D_MODEL),
                   pl.BlockSpec((1, POOL_BUF, POOL_WIDTH), lambda b, c: (b, 0, 0)),
                   pl.BlockSpec((1, RET_HEADS, RET_HEAD_DIM, RET_HEAD_DIM), lambda b, c: (b, 0, 0, 0))],
        out_shape=[jax.ShapeDtypeStruct((batch * seq, D_MODEL), F32),
                   jax.ShapeDtypeStruct((batch, POOL_BUF, POOL_WIDTH), F32),
                   jax.ShapeDtypeStruct((batch, RET_HEADS, RET_HEAD_DIM, RET_HEAD_DIM), F32)],
        scratch_shapes=[pltpu.VMEM((POOL_PAD + C, POOL_WIDTH), F32),
                        pltpu.VMEM((RET_HEADS, RET_HEAD_DIM, RET_HEAD_DIM), F32),
                        pltpu.VMEM((C, D_MODEL), MXU_DTYPE)],
        compiler_params=_params(("parallel", "arbitrary")),
        name="mix_prompt",
    )(proj, x, tabs["cos"], tabs["sin"], tabs["decay"], tabs["zeta"], tabs["xi"], poolw, pscale, wout)


def _mix_sample_kernel(proj_ref, x_ref, buf_ref, s0_ref, cos_ref, sin_ref, decay_ref, zeta_ref, xi_ref,
                       poolw_ref, pscale_ref, wout_ref,
                       y_ref, newbuf_ref, news_ref,
                       ext_ref, mix_ref, qbd_ref, kbd_ref, *, gc, pos0, seq):
    C = RET_CHUNK
    nb = C // seq
    hd = RET_HEAD_DIM

    ext_ref[:, 0:1, :] = jnp.zeros((nb, 1, POOL_WIDTH), F32)
    ext_ref[:, 1:POOL_PAD, :] = buf_ref[...]
    ext_ref[:, POOL_PAD:POOL_PAD + seq, :] = proj_ref[:, 0:POOL_WIDTH].reshape(nb, seq, POOL_WIDTH)
    pos = pos0 + lax.broadcasted_iota(jnp.int32, (nb, seq, POOL_GROUP), 1)
    for g, w in enumerate(POOL_WINDOWS):
        sl = slice(g * POOL_GROUP, (g + 1) * POOL_GROUP)
        tok = ext_ref[:, POOL_PAD:POOL_PAD + seq, sl]
        acc = tok
        for j in range(1, w):
            acc = acc + ext_ref[:, POOL_PAD - j:POOL_PAD - j + seq, sl]
        cnt = jnp.minimum(pos + 1, w).astype(F32)
        pooled = (acc / cnt - tok).reshape(C, POOL_GROUP)
        mixed = jnp.dot(pooled.astype(MXU_DTYPE), poolw_ref[g], preferred_element_type=F32)
        mix_ref[:, sl] = (mixed * pscale_ref[:, sl]).astype(MXU_DTYPE)
    newbuf_ref[...] = ext_ref[:, seq + 1:seq + POOL_PAD, :]

    cosf = cos_ref[...]
    sinf = sin_ref[...]
    row_seq = lax.broadcasted_iota(jnp.int32, (C, hd), 0) // seq
    col_seq = lax.broadcasted_iota(jnp.int32, (hd, C), 1) // seq
    for h in range(RET_HEADS):
        q = _rotary(_head_cols(proj_ref, 0, h), cosf, sinf)
        k = _rotary(_head_cols(proj_ref, 1, h), cosf, sinf) * (hd ** -0.5)
        vb = _head_cols(proj_ref, 2, h).astype(MXU_DTYPE)
        gate = _head_cols(proj_ref, 3, h)
        qb = q.astype(MXU_DTYPE)
        scores = _nt_dot(qb, k.astype(MXU_DTYPE)) * decay_ref[h]
        intra = jnp.dot(scores.astype(MXU_DTYPE), vb, preferred_element_type=F32)
        kzt = (k * zeta_ref[h]).T
        for b in range(nb):
            qbd_ref[:, b * hd:(b + 1) * hd] = jnp.where(row_seq == b, q, 0.0).astype(MXU_DTYPE)
            kbd_ref[b * hd:(b + 1) * hd, :] = jnp.where(col_seq == b, kzt, 0.0).astype(MXU_DTYPE)
        s0 = s0_ref[:, h].reshape(nb * hd, hd)
        cross = jnp.dot(qbd_ref[...], s0.astype(MXU_DTYPE), preferred_element_type=F32) * xi_ref[h]
        kv = jnp.dot(kbd_ref[...], vb, preferred_element_type=F32)
        news_ref[:, h] = (gc[h] * s0 + kv).reshape(nb, hd, hd)
        lo = POOL_WIDTH + h * hd
        mix_ref[:, lo:lo + hd] = _head_norm_gate(intra + cross, gate).astype(MXU_DTYPE)

    y_ref[...] = x_ref[...] + jnp.dot(mix_ref[...], wout_ref[...], preferred_element_type=F32)


def _mix_sample(proj, x, row0, buf, s0, tabs, poolw, pscale, wout, batch, seq):
    C = RET_CHUNK
    nb = C // seq
    hd = RET_HEAD_DIM
    full = lambda shape: pl.BlockSpec(shape, lambda i: (0,) * len(shape))
    kern = functools.partial(_mix_sample_kernel, gc=tabs["gc"], pos0=PAST_LEN, seq=seq)
    return pl.pallas_call(
        kern,
        grid=(batch // nb,),
        in_specs=[pl.BlockSpec((C, IN_COLS), lambda i: (row0 + i, 0)),
                  pl.BlockSpec((C, D_MODEL), lambda i: (row0 + i, 0)),
                  pl.BlockSpec((nb, POOL_BUF, POOL_WIDTH), lambda i: (i, 0, 0)),
                  pl.BlockSpec((nb, RET_HEADS, hd, hd), lambda i: (i, 0, 0, 0)),
                  full((C, hd)), full((C, hd)),
                  full((RET_HEADS, C, C)), full((RET_HEADS, C, hd)), full((RET_HEADS, C, hd)),
                  full((len(POOL_WINDOWS), POOL_GROUP, POOL_GROUP)), full((1, POOL_WIDTH)),
                  full((D_MODEL, D_MODEL))],
        out_specs=[pl.BlockSpec((C, D_MODEL), lambda i: (i, 0)),
                   pl.BlockSpec((nb, POOL_BUF, POOL_WIDTH), lambda i: (i, 0, 0)),
                   pl.BlockSpec((nb, RET_HEADS, hd, hd), lambda i: (i, 0, 0, 0))],
        out_shape=[jax.ShapeDtypeStruct((batch * seq, D_MODEL), F32),
                   jax.ShapeDtypeStruct((batch, POOL_BUF, POOL_WIDTH), F32),
                   jax.ShapeDtypeStruct((batch, RET_HEADS, hd, hd), F32)],
        scratch_shapes=[pltpu.VMEM((nb, POOL_PAD + seq, POOL_WIDTH), F32),
                        pltpu.VMEM((C, D_MODEL), MXU_DTYPE),
                        pltpu.VMEM((C, nb * hd), MXU_DTYPE),
                        pltpu.VMEM((nb * hd, C), MXU_DTYPE)],
        compiler_params=_params(("parallel",)),
        name="mix_sample",
    )(proj, x, buf, s0, tabs["cos"], tabs["sin"], tabs["decay"], tabs["zeta"], tabs["xi"],
      poolw, pscale, wout)


def _pair_candidates():
    pairs = [(i, j) for i in range(PEER_TOPK) for j in range(PEER_TOPK)
             if (i + 1) * (j + 1) <= PEER_TOPK]
    return sorted(pairs, key=lambda ij: ij[0] * PEER_TOPK + ij[1])


def _extract_max(ref, keys, prev_key, n_chains=8):
    n = len(keys)
    per = -(-n // n_chains)
    best = None
    for lo in range(0, n, per):
        m = i = None
        for e in range(lo, min(lo + per, n)):
            v = jnp.where(prev_key == keys[e], NEG_INF, ref[e])
            ref[e] = v
            if m is None:
                m, i = v, jnp.full(v.shape, keys[e], F32)
            else:
                i = jnp.where(v > m, keys[e], i)
                m = jnp.maximum(v, m)
        if best is None:
            best = (m, i)
        else:
            bm, bi = best
            best = (jnp.maximum(m, bm), jnp.where(m > bm, i, bi))
    return best


def _top16(src_ref, keys, val_ref, key_ref):
    def body(it, prev_key):
        m, i = _extract_max(src_ref, keys, prev_key)
        val_ref[it] = m
        key_ref[it] = i
        return i
    lax.fori_loop(0, PEER_TOPK, body, jnp.full(src_ref.shape[1:], -1.0, F32))


def _peer_route_kernel(x_ref, g_ref, wq_ref, keys_ref,
                       t_ref, i1_ref, i2_ref, gate_ref,
                       s_ref, v0_ref, k0_ref, v1_ref, k1_ref, cand_ref, best_ref, flat_ref,
                       e1_ref, e2_ref, gt_ref):
    tm = x_ref.shape[0]
    tb = _rms(x_ref[...], g_ref[...]).astype(MXU_DTYPE)
    t_ref[...] = tb
    q = jnp.dot(tb, wq_ref[...], preferred_element_type=F32).astype(MXU_DTYPE)
    half_w = PEER_HEADS * PEER_HALF
    key_ids = [float(k) for k in range(N_KEYS)]
    for p, (vr, kr) in enumerate(((v0_ref, k0_ref), (v1_ref, k1_ref))):
        s = _nt_dot(keys_ref[p], q[:, p * half_w:(p + 1) * half_w])
        s_ref[...] = s.reshape(N_KEYS, PEER_HEADS, tm)
        _top16(s_ref, key_ids, vr, kr)

    pairs = _pair_candidates()
    for e, (i, j) in enumerate(pairs):
        cand_ref[e] = v0_ref[i] + v1_ref[j]
    _top16(cand_ref, [float(i * PEER_TOPK + j) for i, j in pairs], best_ref, flat_ref)

    top = best_ref[0]
    z = jnp.zeros_like(top)
    for r in range(PEER_TOPK):
        ex = jnp.exp(best_ref[r] - top)
        gt_ref[r] = ex
        z = z + ex
    for r in range(PEER_TOPK):
        gt_ref[r] = gt_ref[r] / z
        flat = flat_ref[r]
        ri = jnp.floor(flat * (1.0 / PEER_TOPK))
        rj = flat - PEER_TOPK * ri
        a1 = jnp.zeros_like(flat)
        a2 = jnp.zeros_like(flat)
        for i in range(PEER_TOPK):
            a1 = jnp.where(ri == float(i), k0_ref[i], a1)
            a2 = jnp.where(rj == float(i), k1_ref[i], a2)
        e1_ref[r] = a1
        e2_ref[r] = a2
    i1_ref[...] = e1_ref[...].reshape(N_TERMS, tm).T
    i2_ref[...] = e2_ref[...].reshape(N_TERMS, tm).T
    gate_ref[...] = gt_ref[...].reshape(N_TERMS, tm).T


def _peer_route(x, g, wq, keys_x):
    t = x.shape[0]
    tm = _tile(t, 256)
    n_cand = len(_pair_candidates())
    slab = lambda n: pltpu.VMEM((n, PEER_HEADS, tm), F32)
    term = jax.ShapeDtypeStruct((t, N_TERMS), F32)
    return pl.pallas_call(
        _peer_route_kernel,
        grid=(t // tm,),
        in_specs=[pl.BlockSpec((tm, D_MODEL), lambda i: (i, 0)),
                  pl.BlockSpec((1, D_MODEL), lambda i: (0, 0)),
                  pl.BlockSpec(wq.shape, lambda i: (0, 0)),
                  pl.BlockSpec(keys_x.shape, lambda i: (0, 0, 0))],
        out_specs=[pl.BlockSpec((tm, D_MODEL), lambda i: (i, 0))]
                  + [pl.BlockSpec((tm, N_TERMS), lambda i: (i, 0))] * 3,
        out_shape=[jax.ShapeDtypeStruct((t, D_MODEL), MXU_DTYPE), term, term, term],
        scratch_shapes=[slab(N_KEYS)] + [slab(PEER_TOPK)] * 4 + [slab(n_cand)] + [slab(PEER_TOPK)] * 5,
        compiler_params=_params(("parallel",)),
        name="peer_route",
    )(x, g, wq, keys_x)


def _peer_wbuild_kernel(i1_ref, i2_ref, gate_ref, w_ref, scr_ref):
    tmb = i1_ref.shape[0]
    key_row = lax.broadcasted_iota(jnp.int32, (N_KEYS, N_TERMS), 0).astype(F32)

    def body(j, carry):
        p1 = jnp.where(i1_ref[pl.ds(j, 1), :] == key_row, gate_ref[pl.ds(j, 1), :], 0.0)
        p2 = jnp.where(i2_ref[pl.ds(j, 1), :] == key_row, 1.0, 0.0)
        wt = _nt_dot(p1.astype(MXU_DTYPE), p2.astype(MXU_DTYPE))
        scr_ref[pl.ds(pl.multiple_of(j * W_PITCH, 8), N_KEYS), :] = wt
        return carry

    lax.fori_loop(0, tmb, body, 0)
    for a in range(N_KEYS):
        w_ref[:, a * N_KEYS:(a + 1) * N_KEYS] = scr_ref[pl.ds(a, tmb, stride=W_PITCH), :].astype(w_ref.dtype)


def _peer_wbuild(i1, i2, gate):
    t = i1.shape[0]
    tmb = _tile(t, 64)
    blk = pl.BlockSpec((tmb, N_TERMS), lambda i: (i, 0))
    return pl.pallas_call(
        _peer_wbuild_kernel,
        grid=(t // tmb,),
        in_specs=[blk, blk, blk],
        out_specs=pl.BlockSpec((tmb, N_KEYS * N_KEYS), lambda i: (i, 0)),
        out_shape=jax.ShapeDtypeStruct((t, N_KEYS * N_KEYS), MXU_DTYPE),
        scratch_shapes=[pltpu.VMEM((tmb * W_PITCH, N_KEYS), F32)],
        compiler_params=_params(("parallel",)),
        name="peer_wbuild",
    )(i1, i2, gate)


def _peer_dense_kernel(t_ref, u_ref, v_ref, w_ref, x_ref, gfin_ref, o_ref, *, final_norm):
    j = pl.program_id(1)

    @pl.when(j == 0)
    def _():
        o_ref[...] = x_ref[...]

    a = _nt_dot(t_ref[...], u_ref[...])
    act = 0.5 * a * (1.0 + lax.erf(a * math.sqrt(0.5)))
    b = (act * w_ref[...].astype(F32)).astype(MXU_DTYPE)
    o_ref[...] += jnp.dot(b, v_ref[...], preferred_element_type=F32)

    if final_norm:
        @pl.when(j == pl.num_programs(1) - 1)
        def _():
            o_ref[...] = _rms(o_ref[...], gfin_ref[...])


def _peer_dense(tb, u, v, w, x, gfin, final_norm):
    t = tb.shape[0]
    ne = u.shape[0]
    tm = _tile(t, 1024)
    te = _tile(ne, 512)
    return pl.pallas_call(
        functools.partial(_peer_dense_kernel, final_norm=final_norm),
        grid=(t // tm, ne // te),
        in_specs=[pl.BlockSpec((tm, D_MODEL), lambda i, j: (i, 0)),
                  pl.BlockSpec((te, D_MODEL), lambda i, j: (j, 0)),
                  pl.BlockSpec((te, D_MODEL), lambda i, j: (j, 0)),
                  pl.BlockSpec((tm, te), lambda i, j: (i, j)),
                  pl.BlockSpec((tm, D_MODEL), lambda i, j: (i, 0)),
                  pl.BlockSpec((1, D_MODEL), lambda i, j: (0, 0))],
        out_specs=pl.BlockSpec((tm, D_MODEL), lambda i, j: (i, 0)),
        out_shape=jax.ShapeDtypeStruct((t, D_MODEL), F32),
        compiler_params=_params(("parallel", "arbitrary")),
        name="peer_dense",
    )(tb, u, v, w, x, gfin)


def _retention_tables(pos, chunk, seq):
    hd = RET_HEAD_DIM
    half = hd // 2
    inv = ROPE_BASE ** (-np.arange(half, dtype=np.float64) / half)
    ang = pos.astype(np.float64)[:, None] * inv[None, :]
    cos, sin = np.cos(ang), np.sin(ang)
    gamma = 1.0 - 2.0 ** (-5.0 - np.arange(RET_HEADS, dtype=np.float64))
    idx = np.arange(chunk)
    t_in = idx % seq
    diff = t_in[:, None] - t_in[None, :]
    same = (idx[:, None] // seq) == (idx[None, :] // seq)
    decay = np.where(same & (diff >= 0), gamma[:, None, None] ** np.maximum(diff, 0)[None], 0.0)
    zeta = gamma[:, None] ** (seq - 1 - t_in)[None, :]
    xi = gamma[:, None] ** (t_in + 1.0)[None, :]
    bcast = lambda a: jnp.asarray(np.broadcast_to(a[:, :, None], a.shape + (hd,)), F32)
    return {
        "cos": jnp.asarray(np.concatenate([cos, cos], axis=1), F32),
        "sin": jnp.asarray(np.concatenate([-sin, sin], axis=1), F32),
        "decay": jnp.asarray(decay, F32),
        "zeta": bcast(zeta),
        "xi": bcast(xi),
        "gc": tuple(float(g) for g in gamma ** seq),
    }


def kernel(x_prompt, x_sample, state_pool, state_ret, norm_mix, w_in, pool_w, pool_scale, w_out,
           norm_ffn, peer_wq, peer_keys, peer_u, peer_v, norm_final):
    bp, lp, d = x_prompt.shape
    bs, ls, _ = x_sample.shape
    depth = w_in.shape[0]
    C = RET_CHUNK
    assert d == D_MODEL and lp % C == 0 and C % ls == 0 and ls % 8 == 0 and bs % (C // ls) == 0
    tp = bp * lp

    tabs_p = _retention_tables(np.arange(lp), C, C)
    tabs_s = _retention_tables(PAST_LEN + np.arange(C) % ls, C, ls)

    x = jnp.concatenate([x_prompt.reshape(tp, d), x_sample.reshape(bs * ls, d)], axis=0)
    eye = jnp.eye(PEER_HEADS, dtype=F32)
    pool_p, ret_p, pool_s, ret_s = [], [], [], []
    for l in range(depth):
        proj = _norm_matmul(x, norm_mix[l][None], w_in[l].astype(MXU_DTYPE))
        poolw = pool_w[l].astype(MXU_DTYPE)
        wout = w_out[l].astype(MXU_DTYPE)
        yp, nbp, nsp = _mix_prompt(proj, x, tabs_p, poolw, pool_scale[l][None], wout, bp, lp)
        ys, nbs, nss = _mix_sample(proj, x, tp // C, state_pool[l], state_ret[l], tabs_s, poolw,
                                   pool_scale[l][None], wout, bs, ls)
        x = jnp.concatenate([yp, ys], axis=0)
        pool_p.append(nbp)
        ret_p.append(nsp)
        pool_s.append(nbs)
        ret_s.append(nss)

        wq = peer_wq[l].reshape(d, PEER_HEADS, 2, PEER_HALF).transpose(0, 2, 1, 3).reshape(d, -1)
        keys_x = jnp.einsum("pkc,hg->pkhgc", peer_keys[l], eye).reshape(
            2, N_KEYS * PEER_HEADS, PEER_HEADS * PEER_HALF)
        tb, i1, i2, gate = _peer_route(x, norm_ffn[l][None], wq.astype(MXU_DTYPE), keys_x.astype(MXU_DTYPE))
        w = _peer_wbuild(i1, i2, gate)
        x = _peer_dense(tb, peer_u[l].astype(MXU_DTYPE), peer_v[l].astype(MXU_DTYPE), w, x,
                        norm_final[None], final_norm=(l == depth - 1))

    y_prompt = x[:tp].reshape(bp, lp, d)
    y_sample = x[tp:].reshape(bs, ls, d)
    return (y_prompt, y_sample, jnp.stack(pool_p), jnp.stack(ret_p), jnp.stack(pool_s), jnp.stack(ret_s))
```

```python
import functools
import math

import numpy as np
import jax
import jax.numpy as jnp
from jax import lax
from jax.experimental import pallas as pl
from jax.experimental.pallas import tpu as pltpu

D_MODEL = 1024
POOL_WIDTH = 512
POOL_WINDOWS = (2, 4, 8, 16)
POOL_GROUP = 128
POOL_BUF = 15
POOL_PAD = 16
RET_WIDTH = 512
RET_HEADS = 4
RET_HEAD_DIM = 128
RET_CHUNK = 128
ROPE_BASE = 10000.0
IN_COLS = POOL_WIDTH + 4 * RET_WIDTH
N_KEYS = 128
PEER_HEADS = 8
PEER_TOPK = 16
PEER_HALF = 128
N_TERMS = PEER_HEADS * PEER_TOPK
EPS = 1e-6
PAST_LEN = 16384

F32 = jnp.float32
MXU_DTYPE = jnp.bfloat16
NEG_INF = float("-inf")
W_PITCH = 129
W_UNROLL = 8
VMEM_LIMIT = 56 * 1024 * 1024


def _tile(n, pref):
    t = min(n, pref)
    while n % t:
        t //= 2
    return t


def _params(sem):
    return pltpu.CompilerParams(dimension_semantics=sem, vmem_limit_bytes=VMEM_LIMIT)


def _rms(x, g):
    ms = jnp.mean(x * x, axis=-1, keepdims=True)
    return x * lax.rsqrt(ms + EPS) * g


def _pair_tile(tp, ts, pref):
    return _tile(math.gcd(tp, ts), pref)


def _pair_specs(tm, n_p, width, **kw):
    return [pl.BlockSpec((tm, width), lambda i, *_: (jnp.minimum(i, n_p - 1), 0), **kw),
            pl.BlockSpec((tm, width), lambda i, *_: (jnp.maximum(i - n_p, 0), 0), **kw)]


def _pair_load(xp_ref, xs_ref, n_p):
    return jnp.where(pl.program_id(0) < n_p, xp_ref[...], xs_ref[...])


def _resident(shape):
    return pl.BlockSpec(shape, lambda *_: (0,) * len(shape), pipeline_mode=pl.Buffered(1))


def _norm_matmul_kernel(xp_ref, xs_ref, g_ref, w_ref, o_ref, *, n_p):
    h = _rms(_pair_load(xp_ref, xs_ref, n_p), g_ref[...])
    o_ref[...] = jnp.dot(h.astype(MXU_DTYPE), w_ref[...], preferred_element_type=F32)


def _norm_matmul(xp, xs, g, w):
    (tp, d), ts = xp.shape, xs.shape[0]
    n = w.shape[1]
    tm = _pair_tile(tp, ts, 512)
    n_p = tp // tm
    return pl.pallas_call(
        functools.partial(_norm_matmul_kernel, n_p=n_p),
        grid=((tp + ts) // tm,),
        in_specs=_pair_specs(tm, n_p, d) + [_resident((1, d)), _resident((d, n))],
        out_specs=pl.BlockSpec((tm, n), lambda i: (i, 0)),
        out_shape=jax.ShapeDtypeStruct((tp + ts, n), F32),
        compiler_params=_params(("parallel",)),
        name="norm_matmul",
    )(xp, xs, g, w)


def _rotary(x, cosf, sinf):
    return x * cosf + pltpu.roll(x, RET_HEAD_DIM // 2, 1) * sinf


def _head_cols(proj_ref, part, h):
    lo = POOL_WIDTH + part * RET_WIDTH + h * RET_HEAD_DIM
    return proj_ref[:, lo:lo + RET_HEAD_DIM]


def _head_norm_gate(o, gate):
    on = o * lax.rsqrt(jnp.mean(o * o, axis=-1, keepdims=True) + EPS)
    return gate * jax.nn.sigmoid(gate) * on


def _nt_dot(a, b):
    return lax.dot_general(a, b, (((1,), (1,)), ((), ())), preferred_element_type=F32)


def _tn_dot(a, b):
    return lax.dot_general(a, b, (((0,), (0,)), ((), ())), preferred_element_type=F32)


def _mix_prompt_kernel(proj_ref, x_ref, cos_ref, sin_ref, decay_ref, zeta_ref, xi_ref,
                       poolw_ref, pscale_ref, wout_ref,
                       y_ref, newbuf_ref, news_ref,
                       ext_ref, s_ref, mix_ref, *, gc, pos0):
    c = pl.program_id(1)
    C = RET_CHUNK

    @pl.when(c == 0)
    def _():
        ext_ref[0:POOL_PAD, :] = jnp.zeros((POOL_PAD, POOL_WIDTH), F32)
        s_ref[...] = jnp.zeros(s_ref.shape, F32)

    ext_ref[POOL_PAD:POOL_PAD + C, :] = proj_ref[:, 0:POOL_WIDTH]
    pos = pos0 + c * C + lax.broadcasted_iota(jnp.int32, (C, POOL_GROUP), 0)
    for g, w in enumerate(POOL_WINDOWS):
        sl = slice(g * POOL_GROUP, (g + 1) * POOL_GROUP)
        tok = ext_ref[POOL_PAD:POOL_PAD + C, sl]
        acc = tok
        for j in range(1, w):
            acc = acc + ext_ref[POOL_PAD - j:POOL_PAD - j + C, sl]
        cnt = jnp.minimum(pos + 1, w).astype(F32)
        pooled = acc / cnt - tok
        mixed = jnp.dot(pooled.astype(MXU_DTYPE), poolw_ref[g], preferred_element_type=F32)
        mix_ref[:, sl] = (mixed * pscale_ref[:, sl]).astype(MXU_DTYPE)
    ext_ref[0:POOL_PAD, :] = ext_ref[C:C + POOL_PAD, :]

    cosf = cos_ref[...]
    sinf = sin_ref[...]
    for h in range(RET_HEADS):
        q = _rotary(_head_cols(proj_ref, 0, h), cosf, sinf)
        k = _rotary(_head_cols(proj_ref, 1, h), cosf, sinf) * (RET_HEAD_DIM ** -0.5)
        vb = _head_cols(proj_ref, 2, h).astype(MXU_DTYPE)
        gate = _head_cols(proj_ref, 3, h)
        qb = q.astype(MXU_DTYPE)
        scores = _nt_dot(qb, k.astype(MXU_DTYPE)) * decay_ref[h]
        intra = jnp.dot(scores.astype(MXU_DTYPE), vb, preferred_element_type=F32)
        s_prev = s_ref[h]
        cross = jnp.dot(qb, s_prev.astype(MXU_DTYPE), preferred_element_type=F32) * xi_ref[h]
        kv = _tn_dot((k * zeta_ref[h]).astype(MXU_DTYPE), vb)
        s_ref[h] = gc[h] * s_prev + kv
        lo = POOL_WIDTH + h * RET_HEAD_DIM
        mix_ref[:, lo:lo + RET_HEAD_DIM] = _head_norm_gate(intra + cross, gate).astype(MXU_DTYPE)

    y_ref[...] = x_ref[...] + jnp.dot(mix_ref[...], wout_ref[...], preferred_element_type=F32)

    @pl.when(c == pl.num_programs(1) - 1)
    def _():
        newbuf_ref[0] = ext_ref[1:POOL_PAD, :]
        news_ref[0] = s_ref[...]


def _mix_prompt(proj, x, tabs, poolw, pscale, wout, batch, seq):
    C = RET_CHUNK
    nc = seq // C
    full = lambda shape: pl.BlockSpec(shape, lambda b, c: (0,) * len(shape))
    row = lambda width: pl.BlockSpec((C, width), lambda b, c: (b * nc + c, 0))
    kern = functools.partial(_mix_prompt_kernel, gc=tabs["gc"], pos0=0)
    return pl.pallas_call(
        kern,
        grid=(batch, nc),
        in_specs=[row(IN_COLS), row(D_MODEL),
                  pl.BlockSpec((C, RET_HEAD_DIM), lambda b, c: (c, 0)),
                  pl.BlockSpec((C, RET_HEAD_DIM), lambda b, c: (c, 0)),
                  full((RET_HEADS, C, C)), full((RET_HEADS, C, RET_HEAD_DIM)),
                  full((RET_HEADS, C, RET_HEAD_DIM)),
                  full((len(POOL_WINDOWS), POOL_GROUP, POOL_GROUP)), full((1, POOL_WIDTH)),
                  full((D_MODEL, D_MODEL))],
        out_specs=[row(D_MODEL),
                   pl.BlockSpec((1, POOL_BUF, POOL_WIDTH), lambda b, c: (b, 0, 0)),
                   pl.BlockSpec((1, RET_HEADS, RET_HEAD_DIM, RET_HEAD_DIM), lambda b, c: (b, 0, 0, 0))],
        out_shape=[jax.ShapeDtypeStruct((batch * seq, D_MODEL), F32),
                   jax.ShapeDtypeStruct((batch, POOL_BUF, POOL_WIDTH), F32),
                   jax.ShapeDtypeStruct((batch, RET_HEADS, RET_HEAD_DIM, RET_HEAD_DIM), F32)],
        scratch_shapes=[pltpu.VMEM((POOL_PAD + C, POOL_WIDTH), F32),
                        pltpu.VMEM((RET_HEADS, RET_HEAD_DIM, RET_HEAD_DIM), F32),
                        pltpu.VMEM((C, D_MODEL), MXU_DTYPE)],
        compiler_params=_params(("parallel", "arbitrary")),
        name="mix_prompt",
    )(proj, x, tabs["cos"], tabs["sin"], tabs["decay"], tabs["zeta"], tabs["xi"], poolw, pscale, wout)


def _mix_sample_kernel(proj_ref, x_ref, buf_ref, s0_ref, cos_ref, sin_ref, decay_ref, zeta_ref, xi_ref,
                       poolw_ref, pscale_ref, wout_ref,
                       y_ref, newbuf_ref, news_ref,
                       ext_ref, mix_ref, qbd_ref, kbd_ref, *, gc, pos0, seq):
    C = RET_CHUNK
    nb = C // seq
    hd = RET_HEAD_DIM

    ext_ref[:, 0:1, :] = jnp.zeros((nb, 1, POOL_WIDTH), F32)
    ext_ref[:, 1:POOL_PAD, :] = buf_ref[...]
    ext_ref[:, POOL_PAD:POOL_PAD + seq, :] = proj_ref[:, 0:POOL_WIDTH].reshape(nb, seq, POOL_WIDTH)
    pos = pos0 + lax.broadcasted_iota(jnp.int32, (nb, seq, POOL_GROUP), 1)
    for g, w in enumerate(POOL_WINDOWS):
        sl = slice(g * POOL_GROUP, (g + 1) * POOL_GROUP)
        tok = ext_ref[:, POOL_PAD:POOL_PAD + seq, sl]
        acc = tok
        for j in range(1, w):
            acc = acc + ext_ref[:, POOL_PAD - j:POOL_PAD - j + seq, sl]
        cnt = jnp.minimum(pos + 1, w).astype(F32)
        pooled = (acc / cnt - tok).reshape(C, POOL_GROUP)
        mixed = jnp.dot(pooled.astype(MXU_DTYPE), poolw_ref[g], preferred_element_type=F32)
        mix_ref[:, sl] = (mixed * pscale_ref[:, sl]).astype(MXU_DTYPE)
    newbuf_ref[...] = ext_ref[:, seq + 1:seq + POOL_PAD, :]

    cosf = cos_ref[...]
    sinf = sin_ref[...]
    row_seq = lax.broadcasted_iota(jnp.int32, (C, hd), 0) // seq
    col_seq = lax.broadcasted_iota(jnp.int32, (hd, C), 1) // seq
    for h in range(RET_HEADS):
        q = _rotary(_head_cols(proj_ref, 0, h), cosf, sinf)
        k = _rotary(_head_cols(proj_ref, 1, h), cosf, sinf) * (hd ** -0.5)
        vb = _head_cols(proj_ref, 2, h).astype(MXU_DTYPE)
        gate = _head_cols(proj_ref, 3, h)
        qb = q.astype(MXU_DTYPE)
        scores = _nt_dot(qb, k.astype(MXU_DTYPE)) * decay_ref[h]
        intra = jnp.dot(scores.astype(MXU_DTYPE), vb, preferred_element_type=F32)
        kzt = (k * zeta_ref[h]).T
        for b in range(nb):
            qbd_ref[:, b * hd:(b + 1) * hd] = jnp.where(row_seq == b, q, 0.0).astype(MXU_DTYPE)
            kbd_ref[b * hd:(b + 1) * hd, :] = jnp.where(col_seq == b, kzt, 0.0).astype(MXU_DTYPE)
        s0 = s0_ref[:, h].reshape(nb * hd, hd)
        cross = jnp.dot(qbd_ref[...], s0.astype(MXU_DTYPE), preferred_element_type=F32) * xi_ref[h]
        kv = jnp.dot(kbd_ref[...], vb, preferred_element_type=F32)
        news_ref[:, h] = (gc[h] * s0 + kv).reshape(nb, hd, hd)
        lo = POOL_WIDTH + h * hd
        mix_ref[:, lo:lo + hd] = _head_norm_gate(intra + cross, gate).astype(MXU_DTYPE)

    y_ref[...] = x_ref[...] + jnp.dot(mix_ref[...], wout_ref[...], preferred_element_type=F32)


def _mix_sample(proj, x, row0, state_pool, state_ret, layer, tabs, poolw, pscale, wout, seq):
    C = RET_CHUNK
    nb = C // seq
    hd = RET_HEAD_DIM
    batch = state_pool.shape[1]
    full = lambda shape: pl.BlockSpec(shape, lambda i: (0,) * len(shape))
    kern = functools.partial(_mix_sample_kernel, gc=tabs["gc"], pos0=PAST_LEN, seq=seq)
    return pl.pallas_call(
        kern,
        grid=(batch // nb,),
        in_specs=[pl.BlockSpec((C, IN_COLS), lambda i: (row0 + i, 0)),
                  pl.BlockSpec((C, D_MODEL), lambda i: (i, 0)),
                  pl.BlockSpec((None, nb, POOL_BUF, POOL_WIDTH), lambda i: (layer, i, 0, 0)),
                  pl.BlockSpec((None, nb, RET_HEADS, hd, hd), lambda i: (layer, i, 0, 0, 0)),
                  full((C, hd)), full((C, hd)),
                  full((RET_HEADS, C, C)), full((RET_HEADS, C, hd)), full((RET_HEADS, C, hd)),
                  full((len(POOL_WINDOWS), POOL_GROUP, POOL_GROUP)), full((1, POOL_WIDTH)),
                  full((D_MODEL, D_MODEL))],
        out_specs=[pl.BlockSpec((C, D_MODEL), lambda i: (i, 0)),
                   pl.BlockSpec((nb, POOL_BUF, POOL_WIDTH), lambda i: (i, 0, 0)),
                   pl.BlockSpec((nb, RET_HEADS, hd, hd), lambda i: (i, 0, 0, 0))],
        out_shape=[jax.ShapeDtypeStruct((batch * seq, D_MODEL), F32),
                   jax.ShapeDtypeStruct((batch, POOL_BUF, POOL_WIDTH), F32),
                   jax.ShapeDtypeStruct((batch, RET_HEADS, hd, hd), F32)],
        scratch_shapes=[pltpu.VMEM((nb, POOL_PAD + seq, POOL_WIDTH), F32),
                        pltpu.VMEM((C, D_MODEL), MXU_DTYPE),
                        pltpu.VMEM((C, nb * hd), MXU_DTYPE),
                        pltpu.VMEM((nb * hd, C), MXU_DTYPE)],
        compiler_params=_params(("parallel",)),
        name="mix_sample",
    )(proj, x, state_pool, state_ret, tabs["cos"], tabs["sin"], tabs["decay"], tabs["zeta"], tabs["xi"],
      poolw, pscale, wout)


def _pair_candidates():
    pairs = [(i, j) for i in range(PEER_TOPK) for j in range(PEER_TOPK)
             if (i + 1) * (j + 1) <= PEER_TOPK]
    return sorted(pairs, key=lambda ij: ij[0] * PEER_TOPK + ij[1])


def _extract_max(ref, keys, prev_key, n_chains=8):
    n = len(keys)
    per = -(-n // n_chains)
    best = None
    for lo in range(0, n, per):
        m = i = None
        for e in range(lo, min(lo + per, n)):
            v = jnp.where(prev_key == keys[e], NEG_INF, ref[e])
            ref[e] = v
            if m is None:
                m, i = v, jnp.full(v.shape, keys[e], F32)
            else:
                i = jnp.where(v > m, keys[e], i)
                m = jnp.maximum(v, m)
        if best is None:
            best = (m, i)
        else:
            bm, bi = best
            best = (jnp.maximum(m, bm), jnp.where(m > bm, i, bi))
    return best


def _top16(src_ref, keys, val_ref, key_ref):
    def body(it, prev_key):
        m, i = _extract_max(src_ref, keys, prev_key)
        val_ref[it] = m
        key_ref[it] = i
        return i
    lax.fori_loop(0, PEER_TOPK, body, jnp.full(src_ref.shape[1:], -1.0, F32))


def _peer_route_kernel(xp_ref, xs_ref, g_ref, wq_ref, keys_ref,
                       t_ref, i1_ref, i2_ref, gate_ref,
                       s_ref, v0_ref, k0_ref, v1_ref, k1_ref, cand_ref, best_ref, flat_ref,
                       e1_ref, e2_ref, gt_ref, *, n_p):
    tm = xp_ref.shape[0]
    tb = _rms(_pair_load(xp_ref, xs_ref, n_p), g_ref[...]).astype(MXU_DTYPE)
    t_ref[...] = tb
    q = jnp.dot(tb, wq_ref[...], preferred_element_type=F32).astype(MXU_DTYPE)
    half_w = PEER_HEADS * PEER_HALF
    key_ids = [float(k) for k in range(N_KEYS)]
    for p, (vr, kr) in enumerate(((v0_ref, k0_ref), (v1_ref, k1_ref))):
        s = _nt_dot(keys_ref[p], q[:, p * half_w:(p + 1) * half_w])
        s_ref[...] = s.reshape(N_KEYS, PEER_HEADS, tm)
        _top16(s_ref, key_ids, vr, kr)

    pairs = _pair_candidates()
    for e, (i, j) in enumerate(pairs):
        cand_ref[e] = v0_ref[i] + v1_ref[j]
    _top16(cand_ref, [float(i * PEER_TOPK + j) for i, j in pairs], best_ref, flat_ref)

    top = best_ref[0]
    z = jnp.zeros_like(top)
    for r in range(PEER_TOPK):
        ex = jnp.exp(best_ref[r] - top)
        gt_ref[r] = ex
        z = z + ex
    for r in range(PEER_TOPK):
        gt_ref[r] = gt_ref[r] / z
        flat = flat_ref[r]
        ri = jnp.floor(flat * (1.0 / PEER_TOPK))
        rj = flat - PEER_TOPK * ri
        a1 = jnp.zeros_like(flat)
        a2 = jnp.zeros_like(flat)
        for i in range(PEER_TOPK):
            a1 = jnp.where(ri == float(i), k0_ref[i], a1)
            a2 = jnp.where(rj == float(i), k1_ref[i], a2)
        e1_ref[r] = a1
        e2_ref[r] = a2
    i1_ref[...] = e1_ref[...].reshape(N_TERMS, tm).T
    i2_ref[...] = e2_ref[...].reshape(N_TERMS, tm).T
    gate_ref[...] = gt_ref[...].reshape(N_TERMS, tm).T


def _peer_route(xp, xs, g, wq, keys_x):
    tp, ts = xp.shape[0], xs.shape[0]
    t = tp + ts
    tm = _pair_tile(tp, ts, 512)
    n_p = tp // tm
    n_cand = len(_pair_candidates())
    slab = lambda n: pltpu.VMEM((n, PEER_HEADS, tm), F32)
    term = jax.ShapeDtypeStruct((t, N_TERMS), F32)
    return pl.pallas_call(
        functools.partial(_peer_route_kernel, n_p=n_p),
        grid=(t // tm,),
        in_specs=_pair_specs(tm, n_p, D_MODEL)
                 + [_resident((1, D_MODEL)), _resident(wq.shape), _resident(keys_x.shape)],
        out_specs=[pl.BlockSpec((tm, D_MODEL), lambda i: (i, 0))]
                  + [pl.BlockSpec((tm, N_TERMS), lambda i: (i, 0))] * 3,
        out_shape=[jax.ShapeDtypeStruct((t, D_MODEL), MXU_DTYPE), term, term, term],
        scratch_shapes=[slab(N_KEYS)] + [slab(PEER_TOPK)] * 4 + [slab(n_cand)] + [slab(PEER_TOPK)] * 5,
        compiler_params=_params(("parallel",)),
        name="peer_route",
    )(xp, xs, g, wq, keys_x)


def _peer_wbuild_kernel(i1_ref, i2_ref, gate_ref, w_ref, scr_ref):
    tmb = i1_ref.shape[0]
    key_row = lax.broadcasted_iota(jnp.int32, (N_KEYS, N_TERMS), 0).astype(F32)

    def body(jb, carry):
        for u in range(W_UNROLL):
            j = jb * W_UNROLL + u
            p1 = jnp.where(i1_ref[pl.ds(j, 1), :] == key_row, gate_ref[pl.ds(j, 1), :], 0.0)
            p2 = jnp.where(i2_ref[pl.ds(j, 1), :] == key_row, 1.0, 0.0)
            wt = _nt_dot(p1.astype(MXU_DTYPE), p2.astype(MXU_DTYPE))
            scr_ref[pl.ds(j * W_PITCH, N_KEYS), :] = wt
        return carry

    lax.fori_loop(0, tmb // W_UNROLL, body, 0)
    for a in range(N_KEYS):
        w_ref[:, a * N_KEYS:(a + 1) * N_KEYS] = scr_ref[pl.ds(a, tmb, stride=W_PITCH), :].astype(w_ref.dtype)


def _peer_wbuild(i1, i2, gate):
    t = i1.shape[0]
    tmb = _tile(t, 64)
    blk = pl.BlockSpec((tmb, N_TERMS), lambda i: (i, 0))
    return pl.pallas_call(
        _peer_wbuild_kernel,
        grid=(t // tmb,),
        in_specs=[blk, blk, blk],
        out_specs=pl.BlockSpec((tmb, N_KEYS * N_KEYS), lambda i: (i, 0)),
        out_shape=jax.ShapeDtypeStruct((t, N_KEYS * N_KEYS), MXU_DTYPE),
        scratch_shapes=[pltpu.VMEM((tmb * W_PITCH, N_KEYS), F32)],
        compiler_params=_params(("parallel",)),
        name="peer_wbuild",
    )(i1, i2, gate)


def _peer_dense_kernel(t_ref, ut_ref, v_ref, w_ref, xp_ref, xs_ref, gfin_ref, op_ref, os_ref, acc_ref,
                       *, n_p, final_norm):
    i = pl.program_id(0)
    j = pl.program_id(1)

    @pl.when(j == 0)
    def _():
        acc_ref[...] = _pair_load(xp_ref, xs_ref, n_p)

    a = jnp.dot(t_ref[...], ut_ref[...], preferred_element_type=F32)
    act = 0.5 * a * (1.0 + lax.erf(a * math.sqrt(0.5)))
    b = (act * w_ref[...].astype(F32)).astype(MXU_DTYPE)
    acc_ref[...] += jnp.dot(b, v_ref[...], preferred_element_type=F32)

    @pl.when(j == pl.num_programs(1) - 1)
    def _():
        y = acc_ref[...]
        if final_norm:
            y = _rms(y, gfin_ref[...])

        @pl.when(i < n_p)
        def _():
            op_ref[...] = y

        @pl.when(i >= n_p)
        def _():
            os_ref[...] = y


def _peer_dense(tb, ut, v, w, xp, xs, gfin, final_norm):
    tp, ts = xp.shape[0], xs.shape[0]
    ne = v.shape[0]
    tm = _pair_tile(tp, ts, 1024)
    te = _tile(ne, 512)
    n_p = tp // tm
    return pl.pallas_call(
        functools.partial(_peer_dense_kernel, n_p=n_p, final_norm=final_norm),
        grid=((tp + ts) // tm, ne // te),
        in_specs=[pl.BlockSpec((tm, D_MODEL), lambda i, j: (i, 0)),
                  pl.BlockSpec((D_MODEL, te), lambda i, j: (0, j)),
                  pl.BlockSpec((te, D_MODEL), lambda i, j: (j, 0)),
                  pl.BlockSpec((tm, te), lambda i, j: (i, j))]
                 + _pair_specs(tm, n_p, D_MODEL, pipeline_mode=pl.Buffered(1))
                 + [_resident((1, D_MODEL))],
        out_specs=_pair_specs(tm, n_p, D_MODEL),
        out_shape=[jax.ShapeDtypeStruct((tp, D_MODEL), F32), jax.ShapeDtypeStruct((ts, D_MODEL), F32)],
        scratch_shapes=[pltpu.VMEM((tm, D_MODEL), F32)],
        compiler_params=_params(("arbitrary", "arbitrary")),
        name="peer_dense",
    )(tb, ut, v, w, xp, xs, gfin)


def _retention_tables(pos, chunk, seq):
    hd = RET_HEAD_DIM
    half = hd // 2
    inv = ROPE_BASE ** (-np.arange(half, dtype=np.float64) / half)
    ang = pos.astype(np.float64)[:, None] * inv[None, :]
    cos, sin = np.cos(ang), np.sin(ang)
    gamma = 1.0 - 2.0 ** (-5.0 - np.arange(RET_HEADS, dtype=np.float64))
    idx = np.arange(chunk)
    t_in = idx % seq
    diff = t_in[:, None] - t_in[None, :]
    same = (idx[:, None] // seq) == (idx[None, :] // seq)
    decay = np.where(same & (diff >= 0), gamma[:, None, None] ** np.maximum(diff, 0)[None], 0.0)
    zeta = gamma[:, None] ** (seq - 1 - t_in)[None, :]
    xi = gamma[:, None] ** (t_in + 1.0)[None, :]
    bcast = lambda a: jnp.asarray(np.broadcast_to(a[:, :, None], a.shape + (hd,)), F32)
    return {
        "cos": jnp.asarray(np.concatenate([cos, cos], axis=1), F32),
        "sin": jnp.asarray(np.concatenate([-sin, sin], axis=1), F32),
        "decay": jnp.asarray(decay, F32),
        "zeta": bcast(zeta),
        "xi": bcast(xi),
        "gc": tuple(float(g) for g in gamma ** seq),
    }


def kernel(x_prompt, x_sample, state_pool, state_ret, norm_mix, w_in, pool_w, pool_scale, w_out,
           norm_ffn, peer_wq, peer_keys, peer_u, peer_v, norm_final):
    bp, lp, d = x_prompt.shape
    bs, ls, _ = x_sample.shape
    depth = w_in.shape[0]
    C = RET_CHUNK
    assert d == D_MODEL and lp % C == 0 and C % ls == 0 and ls % 8 == 0 and bs % (C // ls) == 0
    tp = bp * lp

    tabs_p = _retention_tables(np.arange(lp), C, C)
    tabs_s = _retention_tables(PAST_LEN + np.arange(C) % ls, C, ls)

    xp = x_prompt.reshape(tp, d)
    xs = x_sample.reshape(bs * ls, d)
    eye = jnp.eye(PEER_HEADS, dtype=F32)
    pool_p, ret_p, pool_s, ret_s = [], [], [], []
    for l in range(depth):
        proj = _norm_matmul(xp, xs, norm_mix[l][None], w_in[l].astype(MXU_DTYPE))
        poolw = pool_w[l].astype(MXU_DTYPE)
        wout = w_out[l].astype(MXU_DTYPE)
        yp, nbp, nsp = _mix_prompt(proj, xp, tabs_p, poolw, pool_scale[l][None], wout, bp, lp)
        ys, nbs, nss = _mix_sample(proj, xs, tp // C, state_pool, state_ret, l, tabs_s, poolw,
                                   pool_scale[l][None], wout, ls)
        pool_p.append(nbp)
        ret_p.append(nsp)
        pool_s.append(nbs)
        ret_s.append(nss)

        wq = peer_wq[l].reshape(d, PEER_HEADS, 2, PEER_HALF).transpose(0, 2, 1, 3).reshape(d, -1)
        keys_x = jnp.einsum("pkc,hg->pkhgc", peer_keys[l], eye).reshape(
            2, N_KEYS * PEER_HEADS, PEER_HEADS * PEER_HALF)
        tb, i1, i2, gate = _peer_route(yp, ys, norm_ffn[l][None], wq.astype(MXU_DTYPE),
                                       keys_x.astype(MXU_DTYPE))
        w = _peer_wbuild(i1, i2, gate)
        ut = peer_u[l].astype(MXU_DTYPE).T
        xp, xs = _peer_dense(tb, ut, peer_v[l].astype(MXU_DTYPE), w, yp, ys, norm_final[None],
                             final_norm=(l == depth - 1))

    return (xp.reshape(bp, lp, d), xs.reshape(bs, ls, d),
            jnp.stack(pool_p), jnp.stack(ret_p), jnp.stack(pool_s), jnp.stack(ret_s))
```

```python
import functools
import math

import numpy as np
import jax
import jax.numpy as jnp
from jax import lax
from jax.experimental import pallas as pl
from jax.experimental.pallas import tpu as pltpu

D_MODEL = 1024
POOL_WIDTH = 512
POOL_WINDOWS = (2, 4, 8, 16)
POOL_GROUP = 128
POOL_BUF = 15
POOL_PAD = 16
RET_WIDTH = 512
RET_HEADS = 4
RET_HEAD_DIM = 128
RET_CHUNK = 128
ROPE_BASE = 10000.0
IN_COLS = POOL_WIDTH + 4 * RET_WIDTH
N_KEYS = 128
PEER_HEADS = 8
PEER_TOPK = 16
PEER_HALF = 128
N_TERMS = PEER_HEADS * PEER_TOPK
EPS = 1e-6
PAST_LEN = 16384

F32 = jnp.float32
MXU_DTYPE = jnp.bfloat16
NEG_INF = float("-inf")
W_PITCH = 129
VMEM_LIMIT = 60 * 1024 * 1024


def _tile(n, pref):
    t = min(n, pref)
    while n % t:
        t //= 2
    return t


def _params(sem):
    return pltpu.CompilerParams(dimension_semantics=sem, vmem_limit_bytes=VMEM_LIMIT)


def _rms(x, g):
    ms = jnp.mean(x * x, axis=-1, keepdims=True)
    return x * lax.rsqrt(ms + EPS) * g


def _pair_tile(tp, ts, pref):
    return _tile(math.gcd(tp, ts), pref)


def _pair_specs(tm, n_p, width, sample_mode=None):
    return [pl.BlockSpec((tm, width), lambda i, *_: (jnp.minimum(i, n_p - 1), 0)),
            pl.BlockSpec((tm, width), lambda i, *_: (jnp.maximum(i - n_p, 0), 0), pipeline_mode=sample_mode)]


def _pair_load(xp_ref, xs_ref, n_p):
    return jnp.where(pl.program_id(0) < n_p, xp_ref[...], xs_ref[...])


def _resident(shape):
    return pl.BlockSpec(shape, lambda *_: (0,) * len(shape), pipeline_mode=pl.Buffered(1))


def _norm_matmul_kernel(xp_ref, xs_ref, g_ref, w_ref, o_ref, *, n_p):
    h = _rms(_pair_load(xp_ref, xs_ref, n_p), g_ref[...])
    o_ref[...] = jnp.dot(h.astype(MXU_DTYPE), w_ref[...], preferred_element_type=F32)


def _norm_matmul(xp, xs, g, w):
    (tp, d), ts = xp.shape, xs.shape[0]
    n = w.shape[1]
    tm = _pair_tile(tp, ts, 1024)
    n_p = tp // tm
    return pl.pallas_call(
        functools.partial(_norm_matmul_kernel, n_p=n_p),
        grid=((tp + ts) // tm,),
        in_specs=_pair_specs(tm, n_p, d) + [_resident((1, d)), _resident((d, n))],
        out_specs=pl.BlockSpec((tm, n), lambda i: (i, 0)),
        out_shape=jax.ShapeDtypeStruct((tp + ts, n), F32),
        compiler_params=_params(("parallel",)),
        name="norm_matmul",
    )(xp, xs, g, w)


def _rotary(x, cosf, sinf):
    return x * cosf + pltpu.roll(x, RET_HEAD_DIM // 2, 1) * sinf


def _head_cols(proj_ref, part, h, rows=slice(None)):
    lo = POOL_WIDTH + part * RET_WIDTH + h * RET_HEAD_DIM
    return proj_ref[rows, lo:lo + RET_HEAD_DIM]


def _head_norm_gate(o, gate):
    on = o * lax.rsqrt(jnp.mean(o * o, axis=-1, keepdims=True) + EPS)
    return gate * jax.nn.sigmoid(gate) * on


def _nt_dot(a, b):
    return lax.dot_general(a, b, (((1,), (1,)), ((), ())), preferred_element_type=F32)


def _tn_dot(a, b):
    return lax.dot_general(a, b, (((0,), (0,)), ((), ())), preferred_element_type=F32)


def _mix_prompt_kernel(proj_ref, x_ref, cos_ref, sin_ref, decay_ref, zeta_ref, xi_ref, gc_ref,
                       poolw_ref, pscale_ref, wout_ref,
                       y_ref, newbuf_ref, news_ref,
                       ext_ref, s_ref, mix_ref, *, pos0):
    c = pl.program_id(1)
    C = RET_CHUNK
    R = x_ref.shape[0]

    @pl.when(c == 0)
    def _():
        ext_ref[0:POOL_PAD, :] = jnp.zeros((POOL_PAD, POOL_WIDTH), F32)
        s_ref[...] = jnp.zeros(s_ref.shape, F32)

    ext_ref[POOL_PAD:POOL_PAD + R, :] = proj_ref[:, 0:POOL_WIDTH]
    pos = pos0 + c * R + lax.broadcasted_iota(jnp.int32, (R, POOL_GROUP), 0)
    for g, w in enumerate(POOL_WINDOWS):
        sl = slice(g * POOL_GROUP, (g + 1) * POOL_GROUP)
        tok = ext_ref[POOL_PAD:POOL_PAD + R, sl]
        acc = tok
        for j in range(1, w):
            acc = acc + ext_ref[POOL_PAD - j:POOL_PAD - j + R, sl]
        cnt = jnp.minimum(pos + 1, w).astype(F32)
        pooled = acc / cnt - tok
        mixed = jnp.dot(pooled.astype(MXU_DTYPE), poolw_ref[g], preferred_element_type=F32)
        mix_ref[:, sl] = (mixed * pscale_ref[:, sl]).astype(MXU_DTYPE)
    ext_ref[0:POOL_PAD, :] = ext_ref[R:R + POOL_PAD, :]

    for sub in range(R // C):
        rows = slice(sub * C, (sub + 1) * C)
        cosf = cos_ref[rows, :]
        sinf = sin_ref[rows, :]
        for h in range(RET_HEADS):
            q = _rotary(_head_cols(proj_ref, 0, h, rows), cosf, sinf)
            k = _rotary(_head_cols(proj_ref, 1, h, rows), cosf, sinf) * (RET_HEAD_DIM ** -0.5)
            vb = _head_cols(proj_ref, 2, h, rows).astype(MXU_DTYPE)
            gate = _head_cols(proj_ref, 3, h, rows)
            qb = q.astype(MXU_DTYPE)
            scores = _nt_dot(qb, k.astype(MXU_DTYPE)) * decay_ref[h]
            intra = jnp.dot(scores.astype(MXU_DTYPE), vb, preferred_element_type=F32)
            s_prev = s_ref[h]
            cross = jnp.dot(qb, s_prev.astype(MXU_DTYPE), preferred_element_type=F32) * xi_ref[h]
            kv = _tn_dot((k * zeta_ref[h]).astype(MXU_DTYPE), vb)
            s_ref[h] = gc_ref[h] * s_prev + kv
            lo = POOL_WIDTH + h * RET_HEAD_DIM
            mix_ref[rows, lo:lo + RET_HEAD_DIM] = _head_norm_gate(intra + cross, gate).astype(MXU_DTYPE)

    y_ref[...] = x_ref[...] + jnp.dot(mix_ref[...], wout_ref[...], preferred_element_type=F32)

    @pl.when(c == pl.num_programs(1) - 1)
    def _():
        newbuf_ref[0] = ext_ref[1:POOL_PAD, :]
        news_ref[0] = s_ref[...]


def _mix_prompt(proj, x, tabs, poolw, pscale, wout, batch, seq):
    C = RET_CHUNK
    R = _tile(seq, 8 * C)
    nc = seq // R
    full = lambda shape: pl.BlockSpec(shape, lambda b, c: (0,) * len(shape))
    row = lambda width: pl.BlockSpec((R, width), lambda b, c: (b * nc + c, 0))
    kern = functools.partial(_mix_prompt_kernel, pos0=0)
    return pl.pallas_call(
        kern,
        grid=(batch, nc),
        in_specs=[row(IN_COLS), row(D_MODEL),
                  pl.BlockSpec((R, RET_HEAD_DIM), lambda b, c: (c, 0)),
                  pl.BlockSpec((R, RET_HEAD_DIM), lambda b, c: (c, 0)),
                  full((RET_HEADS, C, C)), full((RET_HEADS, C, RET_HEAD_DIM)),
                  full((RET_HEADS, C, RET_HEAD_DIM)), full((RET_HEADS, 1, RET_HEAD_DIM)),
                  full((len(POOL_WINDOWS), POOL_GROUP, POOL_GROUP)), full((1, POOL_WIDTH)),
                  full((D_MODEL, D_MODEL))],
        out_specs=[row(D_MODEL),
                   pl.BlockSpec((1, POOL_BUF, POOL_WIDTH), lambda b, c: (b, 0, 0)),
                   pl.BlockSpec((1, RET_HEADS, RET_HEAD_DIM, RET_HEAD_DIM), lambda b, c: (b, 0, 0, 0))],
        out_shape=[jax.ShapeDtypeStruct((batch * seq, D_MODEL), F32),
                   jax.ShapeDtypeStruct((batch, POOL_BUF, POOL_WIDTH), F32),
                   jax.ShapeDtypeStruct((batch, RET_HEADS, RET_HEAD_DIM, RET_HEAD_DIM), F32)],
        scratch_shapes=[pltpu.VMEM((POOL_PAD + R, POOL_WIDTH), F32),
                        pltpu.VMEM((RET_HEADS, RET_HEAD_DIM, RET_HEAD_DIM), F32),
                        pltpu.VMEM((R, D_MODEL), MXU_DTYPE)],
        compiler_params=_params(("parallel", "arbitrary")),
        name="mix_prompt",
    )(proj, x, tabs["cos"], tabs["sin"], tabs["decay"], tabs["zeta"], tabs["xi"], tabs["gc"],
      poolw, pscale, wout)


def _mix_sample_kernel(proj_ref, x_ref, buf_ref, s0_ref, cos_ref, sin_ref, decay_ref, zeta_ref, xi_ref,
                       gc_ref, poolw_ref, pscale_ref, wout_ref, *rest, pos0, seq):
    y_ref, newbuf_ref, news_ref, ext_ref, mix_ref, qbd_ref, kbd_ref = rest[-7:]
    C = RET_CHUNK
    nb = C // seq
    hd = RET_HEAD_DIM

    ext_ref[:, 0:1, :] = jnp.zeros((nb, 1, POOL_WIDTH), F32)
    ext_ref[:, 1:POOL_PAD, :] = buf_ref[...]
    ext_ref[:, POOL_PAD:POOL_PAD + seq, :] = proj_ref[:, 0:POOL_WIDTH].reshape(nb, seq, POOL_WIDTH)
    pos = pos0 + lax.broadcasted_iota(jnp.int32, (nb, seq, POOL_GROUP), 1)
    for g, w in enumerate(POOL_WINDOWS):
        sl = slice(g * POOL_GROUP, (g + 1) * POOL_GROUP)
        tok = ext_ref[:, POOL_PAD:POOL_PAD + seq, sl]
        acc = tok
        for j in range(1, w):
            acc = acc + ext_ref[:, POOL_PAD - j:POOL_PAD - j + seq, sl]
        cnt = jnp.minimum(pos + 1, w).astype(F32)
        pooled = (acc / cnt - tok).reshape(C, POOL_GROUP)
        mixed = jnp.dot(pooled.astype(MXU_DTYPE), poolw_ref[g], preferred_element_type=F32)
        mix_ref[:, sl] = (mixed * pscale_ref[:, sl]).astype(MXU_DTYPE)
    newbuf_ref[...] = ext_ref[:, seq + 1:seq + POOL_PAD, :]

    cosf = cos_ref[...]
    sinf = sin_ref[...]
    row_seq = lax.broadcasted_iota(jnp.int32, (C, hd), 0) // seq
    col_seq = lax.broadcasted_iota(jnp.int32, (hd, C), 1) // seq
    for h in range(RET_HEADS):
        q = _rotary(_head_cols(proj_ref, 0, h), cosf, sinf)
        k = _rotary(_head_cols(proj_ref, 1, h), cosf, sinf) * (hd ** -0.5)
        vb = _head_cols(proj_ref, 2, h).astype(MXU_DTYPE)
        gate = _head_cols(proj_ref, 3, h)
        qb = q.astype(MXU_DTYPE)
        scores = _nt_dot(qb, k.astype(MXU_DTYPE)) * decay_ref[h]
        intra = jnp.dot(scores.astype(MXU_DTYPE), vb, preferred_element_type=F32)
        kzt = (k * zeta_ref[h]).T
        for b in range(nb):
            qbd_ref[:, b * hd:(b + 1) * hd] = jnp.where(row_seq == b, q, 0.0).astype(MXU_DTYPE)
            kbd_ref[b * hd:(b + 1) * hd, :] = jnp.where(col_seq == b, kzt, 0.0).astype(MXU_DTYPE)
        s0 = s0_ref[:, h].reshape(nb * hd, hd)
        cross = jnp.dot(qbd_ref[...], s0.astype(MXU_DTYPE), preferred_element_type=F32) * xi_ref[h]
        kv = jnp.dot(kbd_ref[...], vb, preferred_element_type=F32)
        news_ref[:, h] = (gc_ref[h] * s0 + kv).reshape(nb, hd, hd)
        lo = POOL_WIDTH + h * hd
        mix_ref[:, lo:lo + hd] = _head_norm_gate(intra + cross, gate).astype(MXU_DTYPE)

    y_ref[...] = x_ref[...] + jnp.dot(mix_ref[...], wout_ref[...], preferred_element_type=F32)


def _mix_sample(proj, x, row0, state_pool, state_ret, layer, ret_stacked, tabs, poolw, pscale, wout, seq):
    C = RET_CHUNK
    nb = C // seq
    hd = RET_HEAD_DIM
    batch = state_pool.shape[1]
    full = lambda shape: pl.BlockSpec(shape, lambda i: (0,) * len(shape))
    ret_blk = pl.BlockSpec((None, nb, RET_HEADS, hd, hd), lambda i: (layer, i, 0, 0, 0))
    carried = [] if ret_stacked is None else [ret_stacked]
    n_fixed = 13
    kern = functools.partial(_mix_sample_kernel, pos0=PAST_LEN, seq=seq)
    return pl.pallas_call(
        kern,
        grid=(batch // nb,),
        in_specs=[pl.BlockSpec((C, IN_COLS), lambda i: (row0 + i, 0)),
                  pl.BlockSpec((C, D_MODEL), lambda i: (i, 0)),
                  pl.BlockSpec((None, nb, POOL_BUF, POOL_WIDTH), lambda i: (layer, i, 0, 0)),
                  ret_blk,
                  full((C, hd)), full((C, hd)),
                  full((RET_HEADS, C, C)), full((RET_HEADS, C, hd)), full((RET_HEADS, C, hd)),
                  full((RET_HEADS, 1, hd)),
                  full((len(POOL_WINDOWS), POOL_GROUP, POOL_GROUP)), full((1, POOL_WIDTH)),
                  full((D_MODEL, D_MODEL))]
                 + [pl.BlockSpec(memory_space=pl.ANY)] * len(carried),
        out_specs=[pl.BlockSpec((C, D_MODEL), lambda i: (i, 0)),
                   pl.BlockSpec((nb, POOL_BUF, POOL_WIDTH), lambda i: (i, 0, 0)),
                   ret_blk],
        out_shape=[jax.ShapeDtypeStruct((batch * seq, D_MODEL), F32),
                   jax.ShapeDtypeStruct((batch, POOL_BUF, POOL_WIDTH), F32),
                   jax.ShapeDtypeStruct(state_ret.shape, F32)],
        input_output_aliases={n_fixed + k: 2 for k in range(len(carried))},
        scratch_shapes=[pltpu.VMEM((nb, POOL_PAD + seq, POOL_WIDTH), F32),
                        pltpu.VMEM((C, D_MODEL), MXU_DTYPE),
                        pltpu.VMEM((C, nb * hd), MXU_DTYPE),
                        pltpu.VMEM((nb * hd, C), MXU_DTYPE)],
        compiler_params=_params(("parallel",)),
        name="mix_sample",
    )(proj, x, state_pool, state_ret, tabs["cos"], tabs["sin"], tabs["decay"], tabs["zeta"], tabs["xi"],
      tabs["gc"], poolw, pscale, wout, *carried)


def _pair_candidates():
    pairs = [(i, j) for i in range(PEER_TOPK) for j in range(PEER_TOPK)
             if (i + 1) * (j + 1) <= PEER_TOPK]
    return sorted(pairs, key=lambda ij: ij[0] * PEER_TOPK + ij[1])


def _extract_max(ref, keys, prev_key, n_chains=8):
    n = len(keys)
    per = -(-n // n_chains)
    best = None
    for lo in range(0, n, per):
        m = i = None
        for e in range(lo, min(lo + per, n)):
            v = jnp.where(prev_key == keys[e], NEG_INF, ref[e])
            ref[e] = v
            if m is None:
                m, i = v, jnp.full(v.shape, keys[e], F32)
            else:
                i = jnp.where(v > m, keys[e], i)
                m = jnp.maximum(v, m)
        if best is None:
            best = (m, i)
        else:
            bm, bi = best
            best = (jnp.maximum(m, bm), jnp.where(m > bm, i, bi))
    return best


def _top16(src_ref, keys, val_ref, key_ref):
    def body(it, prev_key):
        m, i = _extract_max(src_ref, keys, prev_key)
        val_ref[it] = m
        key_ref[it] = i
        return i
    lax.fori_loop(0, PEER_TOPK, body, jnp.full(src_ref.shape[1:], -1.0, F32))


def _exchange(a, b):
    if a is None or b is None:
        return (b, None) if a is None else (a, None)
    (va, ka), (vb, kb) = a, b
    gt = va > vb
    return ((jnp.maximum(va, vb), jnp.where(gt, ka, kb)),
            (jnp.minimum(va, vb), jnp.where(gt, kb, ka)))


def _bitonic_sort_desc(x):
    x = list(x)
    n = len(x)
    k = 2
    while k <= n:
        j = k // 2
        while j >= 1:
            for i in range(n):
                l = i ^ j
                if l > i:
                    hi, lo = _exchange(x[i], x[l])
                    x[i], x[l] = (hi, lo) if (i & k) == 0 else (lo, hi)
            j //= 2
        k *= 2
    return x


def _merge_top(a, b, dropped):
    n = len(a)
    c = []
    for i in range(n):
        if a[i] is None or b[n - 1 - i] is None:
            c.append(b[n - 1 - i] if a[i] is None else a[i])
            continue
        (va, ka), (vb, kb) = a[i], b[n - 1 - i]
        c.append((jnp.maximum(va, vb), jnp.where(va > vb, ka, kb)))
        dropped = jnp.maximum(dropped, jnp.minimum(va, vb))
    j = n // 2
    while j >= 1:
        for i in range(n):
            l = i ^ j
            if l > i:
                c[i], c[l] = _exchange(c[i], c[l])
        j //= 2
    return c, dropped


def _top16_network(src_ref, keys, val_ref, key_ref):
    g = PEER_TOPK
    n = len(keys)
    n_lists = 1 << (-(-n // g) - 1).bit_length()
    items = [(src_ref[e], keys[e]) for e in range(n)] + [None] * (n_lists * g - n)
    lists = [_bitonic_sort_desc(items[lo:lo + g]) for lo in range(0, n_lists * g, g)]
    dropped = jnp.full(src_ref.shape[1:], NEG_INF, F32)
    while len(lists) > 1:
        merged = []
        for a, b in zip(lists[0::2], lists[1::2]):
            m, dropped = _merge_top(a, b, dropped)
            merged.append(m)
        lists = merged
    top = lists[0]
    tie = jnp.where(top[g - 1][0] > dropped, 0.0, 1.0)
    for r in range(g - 1):
        tie = jnp.where(top[r][0] > top[r + 1][0], tie, 1.0)
    for r in range(g):
        val_ref[r] = top[r][0]
        key_ref[r] = top[r][1]
    return jnp.max(tie)


def _top16_blocks(src_ref, keys, val_ref, key_ref):
    lanes = 128
    pending = []
    for c in range(src_ref.shape[2] // lanes):
        blk = slice(c * lanes, (c + 1) * lanes)
        src, val, key = src_ref.at[:, :, blk], val_ref.at[:, :, blk], key_ref.at[:, :, blk]
        pending.append(((src, val, key), _top16_network(src, keys, val, key)))
    return pending


def _redo_ties(pending, keys):
    for (src, val, key), tie in pending:
        @pl.when(tie > 0.0)
        def _():
            _top16(src, keys, val, key)


def _peer_route_kernel(xp_ref, xs_ref, g_ref, wq_ref, keys_ref,
                       t_ref, i1_ref, i2t_ref, gate_ref,
                       s0_ref, s1_ref, v0_ref, k0_ref, v1_ref, k1_ref, cand_ref, best_ref, flat_ref,
                       e1_ref, e2_ref, gt_ref, *, n_p):
    tm = xp_ref.shape[0]
    tb = _rms(_pair_load(xp_ref, xs_ref, n_p), g_ref[...]).astype(MXU_DTYPE)
    t_ref[...] = tb
    q = jnp.dot(tb, wq_ref[...], preferred_element_type=F32).astype(MXU_DTYPE)
    half_w = PEER_HEADS * PEER_HALF
    key_ids = [float(k) for k in range(N_KEYS)]
    pending = []
    for p, (sr, vr, kr) in enumerate(((s0_ref, v0_ref, k0_ref), (s1_ref, v1_ref, k1_ref))):
        s = _nt_dot(keys_ref[p], q[:, p * half_w:(p + 1) * half_w])
        sr[...] = s.reshape(N_KEYS, PEER_HEADS, tm)
        pending += _top16_blocks(sr, key_ids, vr, kr)
    _redo_ties(pending, key_ids)

    pairs = _pair_candidates()
    flat_ids = [float(i * PEER_TOPK + j) for i, j in pairs]
    for e, (i, j) in enumerate(pairs):
        cand_ref[e] = v0_ref[i] + v1_ref[j]
    _redo_ties(_top16_blocks(cand_ref, flat_ids, best_ref, flat_ref), flat_ids)

    top = best_ref[0]
    z = jnp.zeros_like(top)
    for r in range(PEER_TOPK):
        ex = jnp.exp(best_ref[r] - top)
        gt_ref[r] = ex
        z = z + ex
    for r in range(PEER_TOPK):
        gt_ref[r] = gt_ref[r] / z
        flat = flat_ref[r]
        ri = jnp.floor(flat * (1.0 / PEER_TOPK))
        rj = flat - PEER_TOPK * ri
        a1 = jnp.zeros_like(flat)
        a2 = jnp.zeros_like(flat)
        for i in range(PEER_TOPK):
            a1 = jnp.where(ri == float(i), k0_ref[i], a1)
            a2 = jnp.where(rj == float(i), k1_ref[i], a2)
        e1_ref[r] = a1
        e2_ref[r] = a2
    i1_ref[...] = e1_ref[...].reshape(N_TERMS, tm).T
    i2t_ref[...] = e2_ref[...].reshape(N_TERMS, tm)
    gate_ref[...] = gt_ref[...].reshape(N_TERMS, tm).T


def _peer_route(xp, xs, g, wq, keys_x):
    tp, ts = xp.shape[0], xs.shape[0]
    t = tp + ts
    tm = _pair_tile(tp, ts, 512)
    n_p = tp // tm
    n_cand = len(_pair_candidates())
    slab = lambda n: pltpu.VMEM((n, PEER_HEADS, tm), F32)
    term = jax.ShapeDtypeStruct((t, N_TERMS), F32)
    return pl.pallas_call(
        functools.partial(_peer_route_kernel, n_p=n_p),
        grid=(t // tm,),
        in_specs=_pair_specs(tm, n_p, D_MODEL)
                 + [_resident((1, D_MODEL)), _resident(wq.shape), _resident(keys_x.shape)],
        out_specs=[pl.BlockSpec((tm, D_MODEL), lambda i: (i, 0)),
                   pl.BlockSpec((tm, N_TERMS), lambda i: (i, 0)),
                   pl.BlockSpec((N_TERMS, tm), lambda i: (0, i)),
                   pl.BlockSpec((tm, N_TERMS), lambda i: (i, 0))],
        out_shape=[jax.ShapeDtypeStruct((t, D_MODEL), MXU_DTYPE), term,
                   jax.ShapeDtypeStruct((N_TERMS, t), F32), term],
        scratch_shapes=[slab(N_KEYS)] * 2 + [slab(PEER_TOPK)] * 4 + [slab(n_cand)] + [slab(PEER_TOPK)] * 5,
        compiler_params=_params(("parallel",)),
        name="peer_route",
    )(xp, xs, g, wq, keys_x)


def _peer_wbuild_kernel(i1_ref, i2t_ref, gate_ref, w_ref, scr_ref):
    tmb = i1_ref.shape[0]
    key_row = lax.broadcasted_iota(jnp.int32, (N_KEYS, N_TERMS), 0).astype(F32)
    key_col = lax.broadcasted_iota(jnp.int32, (N_TERMS, N_KEYS), 1).astype(F32).astype(MXU_DTYPE)
    i2t = i2t_ref[...].astype(MXU_DTYPE)
    one = jnp.ones((), MXU_DTYPE)
    zero = jnp.zeros((), MXU_DTYPE)
    for j in range(tmb):
        p1 = jnp.where(i1_ref[j:j + 1, :] == key_row, gate_ref[j:j + 1, :], 0.0)
        p2 = jnp.where(i2t[:, j:j + 1] == key_col, one, zero)
        wt = jnp.dot(p1.astype(MXU_DTYPE), p2, preferred_element_type=F32)
        scr_ref[j * W_PITCH:j * W_PITCH + N_KEYS, :] = wt
    te = w_ref.shape[2]
    for a in range(N_KEYS):
        blk, off = divmod(a * N_KEYS, te)
        w_ref[blk, :, off:off + N_KEYS] = scr_ref[pl.ds(a, tmb, stride=W_PITCH), :].astype(w_ref.dtype)


def _dense_tiles(tp, ts, ne):
    return _pair_tile(tp, ts, 1024), _tile(ne, 1024)


def _peer_wbuild(i1, i2t, gate, tm, te):
    t = i1.shape[0]
    ne = N_KEYS * N_KEYS
    tmb = _tile(tm, 256)
    per = tm // tmb
    blk = pl.BlockSpec((tmb, N_TERMS), lambda k: (k, 0))
    return pl.pallas_call(
        _peer_wbuild_kernel,
        grid=(t // tmb,),
        in_specs=[blk, pl.BlockSpec((N_TERMS, tmb), lambda k: (0, k)), blk],
        out_specs=pl.BlockSpec((None, ne // te, tmb, te), lambda k: (k // per, 0, k % per, 0)),
        out_shape=jax.ShapeDtypeStruct((t // tm, ne // te, tm, te), MXU_DTYPE),
        scratch_shapes=[pltpu.VMEM((tmb * W_PITCH, N_KEYS), F32)],
        compiler_params=_params(("parallel",)),
        name="peer_wbuild",
    )(i1, i2t, gate)


def _peer_dense_kernel(t_ref, u_ref, v_ref, w_ref, xp_ref, xs_ref, gfin_ref, op_ref, os_ref, acc_ref,
                       *, n_p, final_norm):
    i = pl.program_id(0)
    j = pl.program_id(1)

    @pl.when(j == 0)
    def _():
        acc_ref[...] = _pair_load(xp_ref, xs_ref, n_p)

    a = _nt_dot(t_ref[...], u_ref[...])
    act = 0.5 * a * (1.0 + lax.erf(a * math.sqrt(0.5)))
    b = (act * w_ref[...].astype(F32)).astype(MXU_DTYPE)
    acc_ref[...] += jnp.dot(b, v_ref[...], preferred_element_type=F32)

    @pl.when(j == pl.num_programs(1) - 1)
    def _():
        y = acc_ref[...]
        if final_norm:
            y = _rms(y, gfin_ref[...])

        @pl.when(i < n_p)
        def _():
            op_ref[...] = y

        @pl.when(i >= n_p)
        def _():
            os_ref[...] = y


def _peer_dense(tb, u, v, layer, w, xp, xs, gfin, final_norm):
    tp, ts = xp.shape[0], xs.shape[0]
    _, _, tm, te = w.shape
    ne = v.shape[1]
    n_p = tp // tm
    table = pl.BlockSpec((None, te, D_MODEL), lambda i, j: (layer, j, 0))
    return pl.pallas_call(
        functools.partial(_peer_dense_kernel, n_p=n_p, final_norm=final_norm),
        grid=((tp + ts) // tm, ne // te),
        in_specs=[pl.BlockSpec((tm, D_MODEL), lambda i, j: (i, 0)), table, table,
                  pl.BlockSpec((None, None, tm, te), lambda i, j: (i, j, 0, 0))]
                 + _pair_specs(tm, n_p, D_MODEL, sample_mode=pl.Buffered(1))
                 + [_resident((1, D_MODEL))],
        out_specs=_pair_specs(tm, n_p, D_MODEL, sample_mode=pl.Buffered(1)),
        out_shape=[jax.ShapeDtypeStruct((tp, D_MODEL), F32), jax.ShapeDtypeStruct((ts, D_MODEL), F32)],
        scratch_shapes=[pltpu.VMEM((tm, D_MODEL), F32)],
        compiler_params=_params(("arbitrary", "arbitrary")),
        name="peer_dense",
    )(tb, u, v, w, xp, xs, gfin)


def _retention_tables(pos, chunk, seq):
    hd = RET_HEAD_DIM
    half = hd // 2
    inv = ROPE_BASE ** (-jnp.arange(half, dtype=F32) / half)
    ang = jnp.asarray(pos, jnp.int32).astype(F32)[:, None] * inv[None, :]
    cos, sin = jnp.cos(ang), jnp.sin(ang)
    gamma = 1.0 - 2.0 ** (-5.0 - jnp.arange(RET_HEADS, dtype=F32))
    log_g = jnp.log(gamma)
    row = np.arange(chunk)
    t_in = jnp.asarray(row % seq, F32)
    diff = t_in[:, None] - t_in[None, :]
    same = jnp.asarray((row[:, None] // seq) == (row[None, :] // seq))
    decay = jnp.where(same & (diff >= 0), jnp.exp(log_g[:, None, None] * jnp.maximum(diff, 0.0)), 0.0)
    zeta = jnp.exp(log_g[:, None] * (seq - 1 - t_in)[None, :])
    xi = jnp.exp(log_g[:, None] * (t_in + 1.0)[None, :])
    gc = jnp.exp(log_g * seq)
    bcast = lambda a: jnp.broadcast_to(a[:, :, None], a.shape + (hd,))
    return {
        "cos": jnp.concatenate([cos, cos], axis=1),
        "sin": jnp.concatenate([-sin, sin], axis=1),
        "decay": decay,
        "zeta": bcast(zeta),
        "xi": bcast(xi),
        "gc": bcast(gc[:, None]),
    }


def kernel(x_prompt, x_sample, state_pool, state_ret, norm_mix, w_in, pool_w, pool_scale, w_out,
           norm_ffn, peer_wq, peer_keys, peer_u, peer_v, norm_final):
    bp, lp, d = x_prompt.shape
    bs, ls, _ = x_sample.shape
    depth = w_in.shape[0]
    C = RET_CHUNK
    assert d == D_MODEL and lp % C == 0 and C % ls == 0 and ls % 8 == 0 and bs % (C // ls) == 0
    tp = bp * lp

    tabs_p = _retention_tables(np.arange(lp), C, C)
    tabs_s = _retention_tables(PAST_LEN + np.arange(C) % ls, C, ls)

    xp = x_prompt.reshape(tp, d)
    xs = x_sample.reshape(bs * ls, d)
    eye = jnp.eye(PEER_HEADS, dtype=F32)
    u_all = peer_u.astype(MXU_DTYPE)
    v_all = peer_v.astype(MXU_DTYPE)
    tm_dense, te_dense = _dense_tiles(tp, bs * ls, peer_u.shape[1])
    pool_p, ret_p, pool_s = [], [], []
    ret_s = None
    for l in range(depth):
        proj = _norm_matmul(xp, xs, norm_mix[l][None], w_in[l].astype(MXU_DTYPE))
        poolw = pool_w[l].astype(MXU_DTYPE)
        wout = w_out[l].astype(MXU_DTYPE)
        yp, nbp, nsp = _mix_prompt(proj, xp, tabs_p, poolw, pool_scale[l][None], wout, bp, lp)
        ys, nbs, ret_s = _mix_sample(proj, xs, tp // C, state_pool, state_ret, l, ret_s, tabs_s, poolw,
                                     pool_scale[l][None], wout, ls)
        pool_p.append(nbp)
        ret_p.append(nsp)
        pool_s.append(nbs)

        wq = peer_wq[l].reshape(d, PEER_HEADS, 2, PEER_HALF).transpose(0, 2, 1, 3).reshape(d, -1)
        keys_x = jnp.einsum("pkc,hg->pkhgc", peer_keys[l], eye).reshape(
            2, N_KEYS * PEER_HEADS, PEER_HEADS * PEER_HALF)
        tb, i1, i2t, gate = _peer_route(yp, ys, norm_ffn[l][None], wq.astype(MXU_DTYPE),
                                        keys_x.astype(MXU_DTYPE))
        w = _peer_wbuild(i1, i2t, gate, tm_dense, te_dense)
        xp, xs = _peer_dense(tb, u_all, v_all, l, w, yp, ys, norm_final[None],
                             final_norm=(l == depth - 1))

    return (xp.reshape(bp, lp, d), xs.reshape(bs, ls, d),
            jnp.stack(pool_p), jnp.stack(ret_p), jnp.stack(pool_s), ret_s)
```
